```python
import math
import jax, jax.numpy as jnp
from jax import lax
import numpy as np

D_MODEL = 1024
BATCH = 1
SEQ = 16384
DEPTH = 2
DEC_BATCH = 32
DEC_SEQ = 8
PAST_LEN = 16384
PAGE_SIZE = 128

H_A = 4
DK_A = 64
KW_A = 2 * DK_A
DV_A = 2 * DK_A
QW_A = H_A * KW_A
D_ATT = H_A * DV_A
D_CONF = 512
CONF_W = 31
Q_BLOCK = 128
H_C = 16
P_C = 64
D_INNER = H_C * P_C
G_C = 2
HPG_C = H_C // G_C
N_C = 128
D_XBC = D_INNER + 2 * G_C * N_C
SSM_CONV_W = 4
SSD_CHUNK = 128
G_D = 4
DG_D = 128
D_GMLP = G_D * DG_D
CHUNK_D = 128
E_GROUPS = 4
E_PER_GROUP = 8
N_EXPERTS = E_GROUPS * E_PER_GROUP
TOP_K = 2
D_EXPERT = 256
N_EVEN = (DEPTH + 1) // 2
N_ODD = DEPTH // 2
NEG_INF = -1e30

kernel_name = 'hybrid_diffattn_conformer_ssd_gmlp_hmoe_step'


def rmsnorm(x, g, eps=1e-6):
    xf = x.astype(jnp.float32)
    y = xf * lax.rsqrt(jnp.mean(xf * xf, axis=-1, keepdims=True) + eps) * g.astype(jnp.float32)
    return y.astype(x.dtype)


def layernorm(x, g, b, eps=1e-5):
    xf = x.astype(jnp.float32)
    mu = jnp.mean(xf, axis=-1, keepdims=True)
    var = jnp.mean(jnp.square(xf - mu), axis=-1, keepdims=True)
    y = (xf - mu) * lax.rsqrt(var + eps) * g.astype(jnp.float32) + b.astype(jnp.float32)
    return y.astype(x.dtype)


def causal_dwconv(x, prev, w, bias):
    xp = jnp.concatenate([prev.astype(x.dtype), x], axis=1)
    y = lax.conv_general_dilated(xp, w[:, None, :].astype(x.dtype), window_strides=(1,), padding='VALID',
                                 dimension_numbers=('NWC', 'WIO', 'NWC'), feature_group_count=x.shape[-1])
    return y + bias.astype(x.dtype), xp[:, xp.shape[1] - (w.shape[0] - 1):]


def diff_attention(q, k, v, lam, q_offset):
    b, lq = q.shape[:2]
    lk = k.shape[1]
    blk = min(Q_BLOCK, lq)
    nb = -(-lq // blk)
    qp = jnp.pad(q, [(0, 0), (0, nb * blk - lq), (0, 0), (0, 0), (0, 0)])
    qb = jnp.moveaxis(qp.reshape(b, nb, blk, H_A, 2, DK_A), 1, 0)
    pos = (q_offset + jnp.arange(nb * blk)).reshape(nb, blk)
    k_pos = jnp.arange(lk)
    kf = k.astype(jnp.float32)
    vf = v.astype(jnp.float32)
    scale = DK_A ** -0.5

    def one_block(args):
        q_blk, p_blk = args
        s = jnp.einsum('bqhid,bkhid->bhiqk', q_blk.astype(jnp.float32), kf) * scale
        s = jnp.where(k_pos[None, :] <= p_blk[:, None], s, NEG_INF)
        p = jax.nn.softmax(s, axis=-1)
        a = p[:, :, 0] - lam * p[:, :, 1]
        return jnp.einsum('bhqk,bkhe->bqhe', a, vf)

    o = lax.map(one_block, (qb, pos))
    return jnp.moveaxis(o, 0, 1).reshape(b, nb * blk, H_A, DV_A)[:, :lq]


def even_mixer(h, past_k, past_v, conv_prev, lambda_init, w_in, lam_q1, lam_k1, lam_q2, lam_k2, subln_g,
               conv_w, conv_b, ln_g, ln_b, w_out):
    b, L, _ = h.shape
    proj = h @ w_in
    q, k, v, a, gate = jnp.split(proj, [QW_A, 2 * QW_A, 2 * QW_A + D_ATT, 2 * QW_A + D_ATT + D_CONF], axis=-1)
    k_rows = k.reshape(b, L, H_A, KW_A)
    v_rows = v.reshape(b, L, H_A, DV_A)
    if past_k is None:
        k_all, v_all, offset = k_rows, v_rows, 0
    else:
        k_all = jnp.concatenate([past_k.astype(h.dtype), k_rows], axis=1)
        v_all = jnp.concatenate([past_v.astype(h.dtype), v_rows], axis=1)
        offset = past_k.shape[1]
    lam = (jnp.exp(jnp.sum(lam_q1.astype(jnp.float32) * lam_k1.astype(jnp.float32)))
           - jnp.exp(jnp.sum(lam_q2.astype(jnp.float32) * lam_k2.astype(jnp.float32))) + lambda_init)
    o = diff_attention(q.reshape(b, L, H_A, 2, DK_A), k_all.reshape(b, -1, H_A, 2, DK_A), v_all, lam, offset)
    o = rmsnorm(o.astype(h.dtype), subln_g, eps=1e-5) * (1.0 - lambda_init)
    attn_out = o.reshape(b, L, D_ATT)
    glu = a * jax.nn.sigmoid(gate)
    cv, new_conv = causal_dwconv(glu, conv_prev, conv_w, conv_b)
    conv_out = jax.nn.silu(layernorm(cv, ln_g, ln_b))
    out = jnp.concatenate([attn_out, conv_out], axis=-1) @ w_out
    return out, k_rows, v_rows, new_conv


def ssd_scan(x, dt, a, bm, cm, h0):
    b, L = x.shape[:2]
    q = SSD_CHUNK
    pad = (-L) % q
    nc = (L + pad) // q

    def padf(t):
        return jnp.pad(t.astype(jnp.float32), [(0, 0), (0, pad)] + [(0, 0)] * (t.ndim - 2))

    xc = padf(x).reshape(b, nc, q, G_C, HPG_C, P_C)
    dtc = padf(dt).reshape(b, nc, q, G_C, HPG_C)
    bc = padf(bm).reshape(b, nc, q, G_C, N_C)
    cc = padf(cm).reshape(b, nc, q, G_C, N_C)
    cs = jnp.cumsum(dtc * a.astype(jnp.float32).reshape(G_C, HPG_C), axis=2)
    xdt = xc * dtc[..., None]
    causal = jnp.tril(jnp.ones((q, q), bool))[None, None, :, :, None, None]
    seg = cs[:, :, :, None] - cs[:, :, None, :]
    decay = jnp.where(causal, jnp.exp(jnp.where(causal, seg, 0.0)), 0.0)
    cb = jnp.einsum('bclgn,bcsgn->bclsg', cc, bc)
    y_diag = jnp.einsum('bclsgh,bcsghp->bclghp', cb[..., None] * decay, xdt)
    to_end = jnp.exp(cs[:, :, -1:] - cs)
    chunk_states = jnp.einsum('bcsgn,bcsgh,bcsghp->bcghpn', bc, to_end, xdt)
    chunk_decay = jnp.exp(cs[:, :, -1])

    def step(hc, inp):
        st, dec = inp
        return hc * dec[..., None, None] + st, hc

    h_last, h_start = lax.scan(step, h0.astype(jnp.float32).reshape(b, G_C, HPG_C, P_C, N_C),
                               (jnp.moveaxis(chunk_states, 1, 0), jnp.moveaxis(chunk_decay, 1, 0)))
    h_start = jnp.moveaxis(h_start, 0, 1)
    y_off = jnp.einsum('bclgn,bcghpn,bclgh->bclghp', cc, h_start, jnp.exp(cs))
    y = (y_diag + y_off).reshape(b, nc * q, H_C, P_C)[:, :L]
    return y, h_last.reshape(b, H_C, P_C, N_C)


def spatial_gate(vn, ws, bs):
    b, L, _ = vn.shape
    pad = (-L) % CHUNK_D
    nc = (L + pad) // CHUNK_D
    vp = jnp.pad(vn, [(0, 0), (0, pad), (0, 0)]).reshape(b, nc, CHUNK_D, G_D, DG_D)
    wm = jnp.where(jnp.tril(jnp.ones((CHUNK_D, CHUNK_D), bool)), ws, 0.0).astype(vn.dtype)
    s = jnp.einsum('gts,bcsgd->bctgd', wm, vp) + bs.T.astype(vn.dtype)[None, None, :, :, None]
    return s.reshape(b, nc * CHUNK_D, D_GMLP)[:, :L]


def odd_mixer(h, conv_prev, h0, w_in, conv_w, conv_b, dt_bias, a_log, d_skip, norm_g,
              ln_g, ln_b, ws, bs, w_out):
    b, L, _ = h.shape
    proj = h @ w_in
    i1 = D_INNER
    i2 = i1 + D_XBC
    i3 = i2 + H_C
    i4 = i3 + D_GMLP
    z, xbc, dt_raw, u, v = jnp.split(proj, [i1, i2, i3, i4], axis=-1)
    xbc_c, new_conv = causal_dwconv(xbc, conv_prev, conv_w, conv_b)
    xbc_c = jax.nn.silu(xbc_c)
    xs, bm, cm = jnp.split(xbc_c, [D_INNER, D_INNER + G_C * N_C], axis=-1)
    xs = xs.reshape(b, L, H_C, P_C)
    dt = jax.nn.softplus(dt_raw.astype(jnp.float32) + dt_bias.astype(jnp.float32))
    a = -jnp.exp(a_log.astype(jnp.float32))
    y, h_new = ssd_scan(xs, dt, a, bm.reshape(b, L, G_C, N_C), cm.reshape(b, L, G_C, N_C), h0)
    y = y + d_skip.astype(jnp.float32)[:, None] * xs.astype(jnp.float32)
    y = y.reshape(b, L, G_C, D_INNER // G_C) * jax.nn.silu(z.astype(jnp.float32)).reshape(b, L, G_C, D_INNER // G_C)
    y = rmsnorm(y, norm_g.reshape(G_C, D_INNER // G_C), eps=1e-5).reshape(b, L, D_INNER).astype(h.dtype)
    gu = jax.nn.gelu(u)
    vn = layernorm(jax.nn.gelu(v), ln_g, ln_b)
    d_out = gu * spatial_gate(vn, ws, bs)
    out = jnp.concatenate([y, d_out], axis=-1) @ w_out
    tail = L - ((L - 1) // CHUNK_D) * CHUNK_D
    return out, new_conv, h_new, vn[:, L - tail:]


def hier_moe(x, rg_w, rg_b, re_w, re_b, w_gate, w_up, w_down):
    b, L, d = x.shape
    t = x.reshape(-1, d)
    g_logits = (t @ rg_w + rg_b).astype(jnp.float32)
    g_prob = jax.nn.softmax(g_logits, axis=-1)
    g_sel = jnp.argmax(g_logits, axis=-1)
    g_w = jnp.take_along_axis(g_prob, g_sel[:, None], axis=1)[:, 0]
    e_logits = (t @ re_w + re_b).astype(jnp.float32).reshape(-1, E_GROUPS, E_PER_GROUP)
    e_sel = jnp.take_along_axis(e_logits, g_sel[:, None, None], axis=1)[:, 0]
    top_v, top_i = lax.top_k(e_sel, TOP_K)
    top_w = jax.nn.softmax(top_v, axis=-1)
    within = jnp.sum(jax.nn.one_hot(top_i, E_PER_GROUP, dtype=jnp.float32) * top_w[..., None], axis=1)
    combine = (g_w[:, None, None] * jax.nn.one_hot(g_sel, E_GROUPS, dtype=jnp.float32)[:, :, None]
               * within[:, None, :]).reshape(-1, N_EXPERTS).astype(t.dtype)
    y = jnp.zeros_like(t)
    for e in range(N_EXPERTS):
        hid = jax.nn.silu(t @ w_gate[e]) * (t @ w_up[e])
        y = y + combine[:, e:e + 1] * (hid @ w_down[e])
    return y.reshape(b, L, d)


def setup_inputs(seed: int = 0) -> dict:
    key = jax.random.key(seed)
    ks = iter(jax.random.split(key, 64))
    f32 = jnp.float32

    def nrm(shape, scale):
        return jax.random.normal(next(ks), shape, f32) * scale

    def gain(shape):
        return 1.0 + nrm(shape, 0.05)

    n_pages = PAST_LEN // PAGE_SIZE
    n_pool = (DEC_BATCH * n_pages * 5) // 4
    page_table = jax.random.permutation(next(ks), n_pool)[:DEC_BATCH * n_pages].reshape(DEC_BATCH, n_pages).astype(jnp.int32)
    dt0 = jnp.exp(jax.random.uniform(next(ks), (N_ODD, H_C), f32, math.log(1e-3), math.log(1e-1)))
    inp = {
        'x_prompt': nrm((BATCH, SEQ, D_MODEL), 1.0),
        'x_sample': nrm((DEC_BATCH, DEC_SEQ, D_MODEL), 1.0),
        'cache_attn_k': nrm((N_EVEN, n_pool, PAGE_SIZE, H_A, KW_A), 1.0),
        'cache_attn_v': nrm((N_EVEN, n_pool, PAGE_SIZE, H_A, DV_A), 1.0),
        'state_conf_conv': nrm((N_EVEN, DEC_BATCH, CONF_W - 1, D_CONF), 0.5),
        'state_ssm_conv': nrm((N_ODD, DEC_BATCH, SSM_CONV_W - 1, D_XBC), 1.0),
        'state_ssm': nrm((N_ODD, DEC_BATCH, H_C, P_C, N_C), 0.1),
        'page_table': page_table,
        'norm_mix_g': gain((DEPTH, D_MODEL)),
        'norm_moe_g': gain((DEPTH, D_MODEL)),
        'norm_final_g': gain((D_MODEL,)),
        'even_w_in': nrm((N_EVEN, D_MODEL, 2 * QW_A + D_ATT + 2 * D_CONF), D_MODEL ** -0.5),
        'attn_lam_q1': nrm((N_EVEN, DK_A), 0.1),
        'attn_lam_k1': nrm((N_EVEN, DK_A), 0.1),
        'attn_lam_q2': nrm((N_EVEN, DK_A), 0.1),
        'attn_lam_k2': nrm((N_EVEN, DK_A), 0.1),
        'attn_subln_g': gain((N_EVEN, DV_A)),
        'conf_conv_w': nrm((N_EVEN, CONF_W, D_CONF), CONF_W ** -0.5),
        'conf_conv_b': nrm((N_EVEN, D_CONF), 0.02),
        'conf_ln_g': gain((N_EVEN, D_CONF)),
        'conf_ln_b': nrm((N_EVEN, D_CONF), 0.02),
        'even_w_out': nrm((N_EVEN, D_ATT + D_CONF, D_MODEL), (D_ATT + D_CONF) ** -0.5),
        'odd_w_in': nrm((N_ODD, D_MODEL, D_INNER + D_XBC + H_C + 2 * D_GMLP), D_MODEL ** -0.5),
        'ssm_conv_w': nrm((N_ODD, SSM_CONV_W, D_XBC), SSM_CONV_W ** -0.5),
        'ssm_conv_b': nrm((N_ODD, D_XBC), 0.02),
        'ssm_dt_bias': dt0 + jnp.log(-jnp.expm1(-dt0)),
        'ssm_a_log': jnp.log(jax.random.uniform(next(ks), (N_ODD, H_C), f32, 1.0, 16.0)),
        'ssm_d': gain((N_ODD, H_C)),
        'ssm_norm_g': gain((N_ODD, D_INNER)),
        'gmlp_ln_g': gain((N_ODD, D_GMLP)),
        'gmlp_ln_b': nrm((N_ODD, D_GMLP), 0.02),
        'gmlp_ws': nrm((N_ODD, G_D, CHUNK_D, CHUNK_D), CHUNK_D ** -0.5),
        'gmlp_bs': 1.0 + nrm((N_ODD, G_D, CHUNK_D), 0.1),
        'odd_w_out': nrm((N_ODD, D_INNER + D_GMLP, D_MODEL), (D_INNER + D_GMLP) ** -0.5),
        'moe_rg_w': nrm((DEPTH, D_MODEL, E_GROUPS), D_MODEL ** -0.5),
        'moe_rg_b': nrm((DEPTH, E_GROUPS), 0.01),
        'moe_re_w': nrm((DEPTH, D_MODEL, N_EXPERTS), D_MODEL ** -0.5),
        'moe_re_b': nrm((DEPTH, N_EXPERTS), 0.01),
        'moe_w_gate': nrm((DEPTH, N_EXPERTS, D_MODEL, D_EXPERT), D_MODEL ** -0.5),
        'moe_w_up': nrm((DEPTH, N_EXPERTS, D_MODEL, D_EXPERT), D_MODEL ** -0.5),
        'moe_w_down': nrm((DEPTH, N_EXPERTS, D_EXPERT, D_MODEL), D_EXPERT ** -0.5),
    }
    return inp


def reference(x_prompt, x_sample, cache_attn_k, cache_attn_v, state_conf_conv, state_ssm_conv, state_ssm, page_table,
              norm_mix_g, norm_moe_g, norm_final_g,
              even_w_in, attn_lam_q1, attn_lam_k1, attn_lam_q2, attn_lam_k2, attn_subln_g,
              conf_conv_w, conf_conv_b, conf_ln_g, conf_ln_b, even_w_out,
              odd_w_in, ssm_conv_w, ssm_conv_b, ssm_dt_bias, ssm_a_log, ssm_d, ssm_norm_g,
              gmlp_ln_g, gmlp_ln_b, gmlp_ws, gmlp_bs, odd_w_out,
              moe_rg_w, moe_rg_b, moe_re_w, moe_re_b, moe_w_gate, moe_w_up, moe_w_down):
    yp, ys = x_prompt, x_sample
    bp = x_prompt.shape[0]
    db = x_sample.shape[0]
    kp_l, vp_l, cp_l, ks_l, vs_l, cs_l = [], [], [], [], [], []
    scp_l, hp_l, gp_l, scs_l, hs_l, gs_l = [], [], [], [], [], []
    for layer in range(DEPTH):
        hp = rmsnorm(yp, norm_mix_g[layer])
        hs = rmsnorm(ys, norm_mix_g[layer])
        if layer % 2 == 0:
            e = layer // 2
            lambda_init = 0.8 - 0.6 * math.exp(-0.3 * layer)
            w = (even_w_in[e], attn_lam_q1[e], attn_lam_k1[e], attn_lam_q2[e], attn_lam_k2[e], attn_subln_g[e],
                 conf_conv_w[e], conf_conv_b[e], conf_ln_g[e], conf_ln_b[e], even_w_out[e])
            conv0 = jnp.zeros((bp, CONF_W - 1, D_CONF), yp.dtype)
            op, kp, vp, cp = even_mixer(hp, None, None, conv0, lambda_init, *w)
            past_k = cache_attn_k[e, page_table].reshape(db, -1, H_A, KW_A)
            past_v = cache_attn_v[e, page_table].reshape(db, -1, H_A, DV_A)
            os_, ks_, vs_, cs_ = even_mixer(hs, past_k, past_v, state_conf_conv[e], lambda_init, *w)
            kp_l.append(kp); vp_l.append(vp); cp_l.append(cp)
            ks_l.append(ks_); vs_l.append(vs_); cs_l.append(cs_)
        else:
            o = layer // 2
            w = (odd_w_in[o], ssm_conv_w[o], ssm_conv_b[o], ssm_dt_bias[o], ssm_a_log[o], ssm_d[o], ssm_norm_g[o],
                 gmlp_ln_g[o], gmlp_ln_b[o], gmlp_ws[o], gmlp_bs[o], odd_w_out[o])
            conv0 = jnp.zeros((bp, SSM_CONV_W - 1, D_XBC), yp.dtype)
            h0 = jnp.zeros((bp, H_C, P_C, N_C), jnp.float32)
            op, scp, hpn, gp = odd_mixer(hp, conv0, h0, *w)
            os_, scs, hsn, gs = odd_mixer(hs, state_ssm_conv[o], state_ssm[o], *w)
            scp_l.append(scp); hp_l.append(hpn); gp_l.append(gp)
            scs_l.append(scs); hs_l.append(hsn); gs_l.append(gs)
        yp = yp + op
        ys = ys + os_
        mw = (moe_rg_w[layer], moe_rg_b[layer], moe_re_w[layer], moe_re_b[layer],
              moe_w_gate[layer], moe_w_up[layer], moe_w_down[layer])
        yp = yp + hier_moe(rmsnorm(yp, norm_moe_g[layer]), *mw)
        ys = ys + hier_moe(rmsnorm(ys, norm_moe_g[layer]), *mw)
    y_prompt = rmsnorm(yp, norm_final_g)
    y_sample = rmsnorm(ys, norm_final_g)
    new_k_prompt = jnp.stack(kp_l)
    new_v_prompt = jnp.stack(vp_l)
    new_conf_conv_prompt = jnp.stack(cp_l)
    new_ssm_conv_prompt = jnp.stack(scp_l)
    new_ssm_prompt = jnp.stack(hp_l)
    new_gmlp_v_prompt = jnp.stack(gp_l)
    new_k_sample = jnp.stack(ks_l)
    new_v_sample = jnp.stack(vs_l)
    new_conf_conv_sample = jnp.stack(cs_l)
    new_ssm_conv_sample = jnp.stack(scs_l)
    new_ssm_sample = jnp.stack(hs_l)
    new_gmlp_v_sample = jnp.stack(gs_l)
    return (y_prompt, y_sample, new_k_prompt, new_v_prompt, new_conf_conv_prompt, new_ssm_conv_prompt,
            new_ssm_prompt, new_gmlp_v_prompt, new_k_sample, new_v_sample, new_conf_conv_sample,
            new_ssm_conv_sample, new_ssm_sample, new_gmlp_v_sample)
```

```python
import functools
import math

import jax
import jax.numpy as jnp
from jax import lax
from jax.experimental import pallas as pl
from jax.experimental.pallas import tpu as pltpu

F32 = jnp.float32
BF16 = jnp.bfloat16

D_MODEL = 1024
H_A = 4
DK_A = 64
HEAD_W = 128
D_ATT = H_A * HEAD_W
D_CONF = 512
CONF_W = 31
CONF_HALO = 32
H_C = 16
P_C = 64
D_INNER = H_C * P_C
G_C = 2
N_C = 128
D_XBC = D_INNER + 2 * G_C * N_C
SSM_CONV_W = 4
SSM_HALO = 8
CHUNK = 128
G_D = 4
D_GMLP = 512
E_GROUPS = 4
E_PER_GROUP = 8
N_EXPERTS = 32
D_EXPERT = 256
NEG_INF = -1e30
LOG2_E = math.log2(math.e)
LANES = 128
VMEM_LIMIT = 56 * 1024 * 1024


def _params(*sem):
    return pltpu.CompilerParams(dimension_semantics=sem, vmem_limit_bytes=VMEM_LIMIT)


def _sigmoid(x):
    return 1.0 / (1.0 + jnp.exp(-x))


def _silu(x):
    return x * _sigmoid(x)


def _gelu_tanh(x):
    return 0.5 * x * (1.0 + jnp.tanh(math.sqrt(2.0 / math.pi) * (x + 0.044715 * (x * x * x))))


def _softplus(x):
    return jnp.maximum(x, 0.0) + jnp.log(1.0 + jnp.exp(-jnp.abs(x)))


def _rms(x, g, eps):
    return x * lax.rsqrt(jnp.mean(x * x, axis=-1, keepdims=True) + eps) * g


def _dot(a, b):
    return jnp.dot(a, b, preferred_element_type=F32)


def _dot_nt(a, b):
    return lax.dot_general(a, b, (((1,), (1,)), ((), ())), preferred_element_type=F32)


def _row_tile(t, cap):
    tm = min(t, cap)
    while t % tm:
        tm //= 2
    return tm


def _hi_f32(x):
    return x.astype(BF16).astype(F32)


def _split2(x):
    hi = x.astype(BF16)
    return hi, (x - hi.astype(F32)).astype(BF16)


def _dot3(a_hi, a_lo, w_hi, w_lo):
    return _dot(a_hi, w_hi) + _dot(a_hi, w_lo) + _dot(a_lo, w_hi)


def _split_weight_body(w_ref, hi_ref, lo_ref):
    hi, lo = _split2(w_ref[...])
    hi_ref[...] = hi
    lo_ref[...] = lo


def _split_weight(w):
    r, c = w.shape
    tr = _row_tile(r, 256)
    spec = pl.BlockSpec((tr, c), lambda i: (i, 0))
    return pl.pallas_call(
        _split_weight_body,
        grid=(r // tr,),
        in_specs=[spec],
        out_specs=[spec, spec],
        out_shape=[jax.ShapeDtypeStruct((r, c), BF16), jax.ShapeDtypeStruct((r, c), BF16)],
        compiler_params=_params("parallel"),
        name="split_weight",
    )(w)


def _even_in_body(x_ref, g_ref, wh_ref, wl_ref, qx_ref, kx_ref, k_ref, v_ref, vx_ref, glu_ref):
    h_hi, h_lo = _split2(_rms(x_ref[...], g_ref[...], 1e-6))

    def proj(c0, n):
        return _dot3(h_hi, h_lo, wh_ref[:, c0:c0 + n], wl_ref[:, c0:c0 + n])

    q = proj(0, D_ATT) * (DK_A ** -0.5 * LOG2_E)
    k = proj(D_ATT, D_ATT)
    k_ref[...] = k
    first = lax.broadcasted_iota(jnp.int32, (q.shape[0], HEAD_W), 1) < DK_A
    for hd in range(H_A):
        for src, dst, is_q in ((q, qx_ref, True), (k, kx_ref, False)):
            blk = src[:, hd * HEAD_W:(hd + 1) * HEAD_W]
            rot = pltpu.roll(blk, DK_A, 1)
            bh, rh = _hi_f32(blk), _hi_f32(rot)
            bl, rl = blk - bh, rot - rh
            lo_a = hd * 2 * HEAD_W
            lo_b = lo_a + HEAD_W
            if is_q:
                parts = ((jnp.where(first, bh, rh), jnp.where(first, bl, 0.0)),
                         (jnp.where(first, rh, bh), jnp.where(first, rl, 0.0)))
            else:
                parts = ((jnp.where(first, bh, rl), jnp.where(first, bh, 0.0)),
                         (jnp.where(first, rh, bl), jnp.where(first, rh, 0.0)))
            for i in range(2):
                dst[i, :, lo_a:lo_b] = parts[i][0].astype(BF16)
                dst[i, :, lo_b:lo_b + HEAD_W] = parts[i][1].astype(BF16)
    v = proj(2 * D_ATT, D_ATT)
    v_ref[...] = v
    v_hi, v_lo = _split2(v)
    for hd in range(H_A):
        vx_ref[:, 2 * hd * HEAD_W:(2 * hd + 1) * HEAD_W] = v_hi[:, hd * HEAD_W:(hd + 1) * HEAD_W]
        vx_ref[:, (2 * hd + 1) * HEAD_W:(2 * hd + 2) * HEAD_W] = v_lo[:, hd * HEAD_W:(hd + 1) * HEAD_W]
    a = proj(3 * D_ATT, D_CONF)
    gate = proj(3 * D_ATT + D_CONF, D_CONF)
    glu_ref[...] = a * _sigmoid(gate)


def _even_in(x, g, wh, wl):
    t = x.shape[0]
    tm = _row_tile(t, 256)
    row = lambda i: (i, 0)
    full = lambda i: (0, 0)
    wide = lambda i: (0, i, 0)
    return pl.pallas_call(
        _even_in_body,
        grid=(t // tm,),
        in_specs=[pl.BlockSpec((tm, D_MODEL), row), pl.BlockSpec((1, D_MODEL), full),
                  pl.BlockSpec(wh.shape, full), pl.BlockSpec(wl.shape, full)],
        out_specs=[pl.BlockSpec((2, tm, 2 * D_ATT), wide), pl.BlockSpec((2, tm, 2 * D_ATT), wide),
                   pl.BlockSpec((tm, D_ATT), row), pl.BlockSpec((tm, D_ATT), row),
                   pl.BlockSpec((tm, 2 * D_ATT), row), pl.BlockSpec((tm, D_CONF), row)],
        out_shape=[jax.ShapeDtypeStruct((2, t, 2 * D_ATT), BF16), jax.ShapeDtypeStruct((2, t, 2 * D_ATT), BF16),
                   jax.ShapeDtypeStruct((t, D_ATT), F32), jax.ShapeDtypeStruct((t, D_ATT), F32),
                   jax.ShapeDtypeStruct((t, 2 * D_ATT), BF16), jax.ShapeDtypeStruct((t, D_CONF), F32)],
        compiler_params=_params("parallel"),
        name="even_in",
    )(x, g, wh, wl)


def _lambda(lamv_ref, lambda_init):
    t1 = jnp.sum(lamv_ref[0:1, :] * lamv_ref[1:2, :], axis=1, keepdims=True)
    t2 = jnp.sum(lamv_ref[2:3, :] * lamv_ref[3:4, :], axis=1, keepdims=True)
    return jnp.exp(t1) - jnp.exp(t2) + lambda_init


def _online_update(s, v_tiles, m_scr, l_scr, acc_scr):
    rows = s.shape[0]
    m_prev = m_scr[...]
    m_new = jnp.maximum(m_prev, jnp.max(s, axis=1, keepdims=True))
    alpha = jnp.exp2(m_prev - m_new)
    p = jnp.exp2(s - m_new)
    l_scr[...] = alpha * l_scr[...] + jnp.sum(p, axis=1, keepdims=True)
    p_hi, p_lo = _split2(p)
    p_hl = jnp.concatenate([p_hi, p_lo], axis=0)
    pv = None
    for off, width, vt in v_tiles:
        v_hi, v_lo = _split2(vt)
        both = _dot(p_hl[:, off:off + width], v_hi)
        d = both[:rows] + both[rows:] + _dot(p_hi[:, off:off + width], v_lo)
        pv = d if pv is None else pv + d
    acc_scr[...] = alpha * acc_scr[...] + pv
    m_scr[...] = m_new


def _attn_prompt_body(lamv_ref, q_ref, k_ref, v_ref, g_ref, o_ref, m_scr, l_scr, acc_scr, *, tq, rc, lambda_init):
    i = pl.program_id(1)
    m_scr[...] = jnp.full(m_scr.shape, NEG_INF, F32)
    l_scr[...] = jnp.zeros(l_scr.shape, F32)
    acc_scr[...] = jnp.zeros(acc_scr.shape, F32)

    def tile(j, masked):
        start = pl.multiple_of(j * tq, tq)
        vx = v_ref[pl.ds(start, tq), :]
        for mp in range(2):
            kt = k_ref[mp, pl.ds(start, tq), :]
            for c in range(tq // rc):
                r0 = mp * tq + c * rc
                s = _dot_nt(q_ref[mp, c * rc:(c + 1) * rc, :], kt)
                if masked:
                    row = lax.broadcasted_iota(jnp.int32, s.shape, 0) + c * rc
                    col = lax.broadcasted_iota(jnp.int32, s.shape, 1)
                    s = jnp.where(col <= row, s, NEG_INF)
                tiles = [s[:, b * LANES:(b + 1) * LANES] for b in range(tq // LANES)]
                top = functools.reduce(jnp.maximum, tiles)
                m_prev = m_scr[r0:r0 + rc, :]
                m_new = jnp.maximum(m_prev, jnp.max(top, axis=1, keepdims=True))
                alpha = jnp.exp2(m_prev - m_new)
                ps = [jnp.exp2(t - m_new) for t in tiles]
                l_scr[r0:r0 + rc, :] = (alpha * l_scr[r0:r0 + rc, :]
                                        + jnp.sum(functools.reduce(jnp.add, ps), axis=1, keepdims=True))
                p_hi, p_lo = _split2(jnp.concatenate(ps, axis=1))
                pv = _dot(p_hi, vx)
                acc_scr[r0:r0 + rc, :] = (alpha * acc_scr[r0:r0 + rc, :] + pv[:, :HEAD_W] + pv[:, HEAD_W:]
                                          + _dot(p_lo, vx[:, :HEAD_W]))
                m_scr[r0:r0 + rc, :] = m_new

    def body(j, carry):
        tile(j, False)
        return carry

    lax.fori_loop(0, i, body, 0)
    tile(i, True)
    lam = _lambda(lamv_ref, lambda_init)
    o1 = acc_scr[0:tq, :] / l_scr[0:tq, :]
    o2 = acc_scr[tq:, :] / l_scr[tq:, :]
    o_ref[...] = _rms(o1 - lam * o2, g_ref[...], 1e-5) * (1.0 - lambda_init)


def _attn_prompt(lamv, qx, kx, vx, subln_g, lambda_init):
    t = vx.shape[0]
    tq = _row_tile(t, 512)
    rc = _row_tile(tq, 256)
    body = functools.partial(_attn_prompt_body, tq=tq, rc=rc, lambda_init=lambda_init)
    return pl.pallas_call(
        body,
        grid=(H_A, t // tq),
        in_specs=[pl.BlockSpec((8, LANES), lambda h, i: (0, 0)),
                  pl.BlockSpec((2, tq, 2 * HEAD_W), lambda h, i: (0, i, h)),
                  pl.BlockSpec((2, t, 2 * HEAD_W), lambda h, i: (0, 0, h)),
                  pl.BlockSpec((t, 2 * HEAD_W), lambda h, i: (0, h)),
                  pl.BlockSpec((1, HEAD_W), lambda h, i: (0, 0))],
        out_specs=pl.BlockSpec((tq, HEAD_W), lambda h, i: (i, h)),
        out_shape=jax.ShapeDtypeStruct((t, D_ATT), F32),
        scratch_shapes=[pltpu.VMEM((2 * tq, LANES), F32), pltpu.VMEM((2 * tq, LANES), F32),
                        pltpu.VMEM((2 * tq, HEAD_W), F32)],
        compiler_params=_params("parallel", "arbitrary"),
        name="attn_prompt",
    )(lamv, qx, kx, vx, subln_g)


def _attn_sample_body(pt_ref, lamv_ref, qh_ref, ql_ref, kn_ref, vn_ref, g_ref, *rest, n_pg, page, lq, lambda_init):
    k_refs = rest[:n_pg]
    v_refs = rest[n_pg:2 * n_pg]
    o_ref, m_scr, l_scr, acc_scr = rest[2 * n_pg:]
    j = pl.program_id(1)
    rows = 2 * H_A * lq

    @pl.when(j == 0)
    def _():
        m_scr[...] = jnp.full(m_scr.shape, NEG_INF, F32)
        l_scr[...] = jnp.zeros(l_scr.shape, F32)
        acc_scr[...] = jnp.zeros(acc_scr.shape, F32)

    rowh = (lax.broadcasted_iota(jnp.int32, (H_A * lq, D_ATT), 0) // lq)
    colh = lax.broadcasted_iota(jnp.int32, (H_A * lq, D_ATT), 1) // HEAD_W
    blocks = []
    for ref in (qh_ref, ql_ref):
        for i in range(2):
            rep = jnp.concatenate([ref[i]] * H_A, axis=0)
            blocks.append(jnp.where(rowh == colh, rep, 0.0))
    qhl = jnp.concatenate(blocks, axis=0).astype(BF16)
    qh = qhl[:rows]

    def scores(k):
        kh, kl = _split2(k)
        both = _dot_nt(qhl, kh)
        return both[:rows] + both[rows:] + _dot_nt(qh, kl)

    s = jnp.concatenate([scores(k_refs[p][...]) for p in range(n_pg)], axis=1)
    _online_update(s, [(p * page, page, v_refs[p][...]) for p in range(n_pg)],
                   m_scr, l_scr, acc_scr)

    @pl.when(j == pl.num_programs(1) - 1)
    def _():
        pad = jnp.zeros((page - lq, D_ATT), F32)
        vn = jnp.concatenate([vn_ref[...], pad], axis=0)
        sn = scores(jnp.concatenate([kn_ref[...], pad], axis=0))
        qpos = lax.broadcasted_iota(jnp.int32, sn.shape, 0) % lq
        kpos = lax.broadcasted_iota(jnp.int32, sn.shape, 1)
        sn = jnp.where(kpos <= qpos, sn, NEG_INF)
        _online_update(sn, [(0, page, vn)], m_scr, l_scr, acc_scr)
        lam = _lambda(lamv_ref, lambda_init)
        o = acc_scr[...] / l_scr[...]
        d = o[:H_A * lq] - lam * o[H_A * lq:]
        outs = []
        for h in range(H_A):
            dh = d[h * lq:(h + 1) * lq, h * HEAD_W:(h + 1) * HEAD_W]
            outs.append(_rms(dh, g_ref[...], 1e-5) * (1.0 - lambda_init))
        o_ref[...] = jnp.concatenate(outs, axis=1)


def _attn_sample(page_table, lamv, q_hi, q_lo, k_new, v_new, subln_g, cache_k, cache_v, lq, lambda_init):
    nb, n_pages = page_table.shape
    page = cache_k.shape[1]
    n_pg = math.gcd(n_pages, 8)
    rows = 2 * H_A * lq
    body = functools.partial(_attn_sample_body, n_pg=n_pg, page=page, lq=lq, lambda_init=lambda_init)

    def page_spec(p):
        return pl.BlockSpec((None, page, D_ATT), lambda b, j, pt, p=p: (pt[b, j * n_pg + p], 0, 0))

    grid_spec = pltpu.PrefetchScalarGridSpec(
        num_scalar_prefetch=1,
        grid=(nb, n_pages // n_pg),
        in_specs=[pl.BlockSpec((8, LANES), lambda b, j, pt: (0, 0)),
                  pl.BlockSpec((2, lq, D_ATT), lambda b, j, pt: (0, b, 0)),
                  pl.BlockSpec((2, lq, D_ATT), lambda b, j, pt: (0, b, 0)),
                  pl.BlockSpec((lq, D_ATT), lambda b, j, pt: (b, 0)),
                  pl.BlockSpec((lq, D_ATT), lambda b, j, pt: (b, 0)),
                  pl.BlockSpec((1, HEAD_W), lambda b, j, pt: (0, 0))]
                 + [page_spec(p) for p in range(n_pg)] + [page_spec(p) for p in range(n_pg)],
        out_specs=pl.BlockSpec((lq, D_ATT), lambda b, j, pt: (b, 0)),
        scratch_shapes=[pltpu.VMEM((rows, 1), F32), pltpu.VMEM((rows, 1), F32),
                        pltpu.VMEM((rows, D_ATT), F32)],
    )
    return pl.pallas_call(
        body,
        grid_spec=grid_spec,
        out_shape=jax.ShapeDtypeStruct((nb * lq, D_ATT), F32),
        compiler_params=_params("parallel", "arbitrary"),
        name="attn_sample",
    )(page_table, lamv, q_hi, q_lo, k_new, v_new, subln_g, *([cache_k] * n_pg), *([cache_v] * n_pg))


def _conf_conv_body(x_ref, halo_ref, w_ref, b_ref, g_ref, beta_ref, o_ref, xp_scr, *, tt, rb, halo_is_x):
    halo = halo_ref[...]
    if halo_is_x:
        halo = jnp.where(pl.program_id(0) == 0, 0.0, halo)
    xp_scr[0:CONF_HALO, :] = halo
    xp_scr[CONF_HALO:CONF_HALO + tt, :] = x_ref[...]
    first = CONF_HALO - (CONF_W - 1)
    for r0 in range(0, tt, rb):
        acc = jnp.broadcast_to(b_ref[...], (rb, D_CONF))
        for j in range(CONF_W):
            acc = acc + w_ref[j:j + 1, :] * xp_scr[r0 + first + j:r0 + first + j + rb, :]
        mu = jnp.mean(acc, axis=-1, keepdims=True)
        cen = acc - mu
        var = jnp.mean(cen * cen, axis=-1, keepdims=True)
        y = cen * lax.rsqrt(var + 1e-5) * g_ref[...] + beta_ref[...]
        o_ref[r0:r0 + rb, :] = _silu(y)


def _conf_conv(x, halo_src, w, b, g, beta, tt, halo_is_x):
    t = x.shape[0]
    rb = min(tt, 64)
    body = functools.partial(_conf_conv_body, tt=tt, rb=rb, halo_is_x=halo_is_x)
    if halo_is_x:
        per = tt // CONF_HALO
        halo_spec = pl.BlockSpec((CONF_HALO, D_CONF), lambda i: (jnp.maximum(i * per - 1, 0), 0))
    else:
        halo_spec = pl.BlockSpec((None, CONF_HALO, D_CONF), lambda i: (i, 0, 0))
    full = lambda i: (0, 0)
    return pl.pallas_call(
        body,
        grid=(t // tt,),
        in_specs=[pl.BlockSpec((tt, D_CONF), lambda i: (i, 0)), halo_spec,
                  pl.BlockSpec(w.shape, full), pl.BlockSpec((1, D_CONF), full),
                  pl.BlockSpec((1, D_CONF), full), pl.BlockSpec((1, D_CONF), full)],
        out_specs=pl.BlockSpec((tt, D_CONF), lambda i: (i, 0)),
        out_shape=jax.ShapeDtypeStruct((t, D_CONF), F32),
        scratch_shapes=[pltpu.VMEM((CONF_HALO + tt, D_CONF), F32)],
        compiler_params=_params("parallel"),
        name="conf_conv",
    )(x, halo_src, w, b, g, beta)


def _out_proj_body(a_ref, c_ref, res_ref, wa_ref, wc_ref, o_ref):
    o_ref[...] = (res_ref[...] + _dot(a_ref[...].astype(BF16), wa_ref[...])
                  + _dot(c_ref[...].astype(BF16), wc_ref[...]))


def _out_proj_split_body(a_ref, c_ref, res_ref, wah_ref, wal_ref, wch_ref, wcl_ref, o_ref):
    a_hi, a_lo = _split2(a_ref[...])
    c_hi, c_lo = _split2(c_ref[...])
    o_ref[...] = (res_ref[...] + _dot3(a_hi, a_lo, wah_ref[...], wal_ref[...])
                  + _dot3(c_hi, c_lo, wch_ref[...], wcl_ref[...]))


def _out_proj(a, c, res, *weights):
    t = a.shape[0]
    tm = _row_tile(t, 512)
    row = lambda i: (i, 0)
    full = lambda i: (0, 0)
    body = _out_proj_body if len(weights) == 2 else _out_proj_split_body
    return pl.pallas_call(
        body,
        grid=(t // tm,),
        in_specs=[pl.BlockSpec((tm, a.shape[1]), row), pl.BlockSpec((tm, c.shape[1]), row),
                  pl.BlockSpec((tm, D_MODEL), row)] + [pl.BlockSpec(w.shape, full) for w in weights],
        out_specs=pl.BlockSpec((tm, D_MODEL), row),
        out_shape=jax.ShapeDtypeStruct((t, D_MODEL), F32),
        compiler_params=_params("parallel"),
        name="out_proj",
    )(a, c, res, *weights)


_O_Z = 0
_O_XBC = _O_Z + D_INNER
_O_U = _O_XBC + D_XBC
_O_V = _O_U + D_GMLP
_O_DTE = _O_V + D_GMLP
_O_DT = _O_DTE + D_INNER
_O_END = _O_DT + LANES


def _odd_in_body(x_ref, g_ref, w_ref, z_ref, xbc_ref, u_ref, v_ref, dte_ref, dt_ref):
    h = _rms(x_ref[...], g_ref[...], 1e-6).astype(BF16)
    z_ref[...] = _dot(h, w_ref[:, _O_Z:_O_XBC])
    xbc_ref[...] = _dot(h, w_ref[:, _O_XBC:_O_U])
    u_ref[...] = _dot(h, w_ref[:, _O_U:_O_V])
    v_ref[...] = _dot(h, w_ref[:, _O_V:_O_DTE])
    dte_ref[...] = _dot(h, w_ref[:, _O_DTE:_O_DT])
    dt_ref[...] = _dot(h, w_ref[:, _O_DT:_O_END])


def _odd_in(x, g, w):
    t = x.shape[0]
    tm = _row_tile(t, 512)
    row = lambda i: (i, 0)
    full = lambda i: (0, 0)
    widths = (D_INNER, D_XBC, D_GMLP, D_GMLP, D_INNER, LANES)
    return pl.pallas_call(
        _odd_in_body,
        grid=(t // tm,),
        in_specs=[pl.BlockSpec((tm, D_MODEL), row), pl.BlockSpec((1, D_MODEL), full),
                  pl.BlockSpec(w.shape, full)],
        out_specs=[pl.BlockSpec((tm, n), row) for n in widths],
        out_shape=[jax.ShapeDtypeStruct((t, n), F32) for n in widths],
        compiler_params=_params("parallel"),
        name="odd_in",
    )(x, g, w)


def _split3(x):
    hi = x.astype(BF16)
    r1 = x - hi.astype(F32)
    mid = r1.astype(BF16)
    lo = (r1 - mid.astype(F32)).astype(BF16)
    return hi, mid, lo


def _cumsum_rows(tril_b, x):
    hi, mid, lo = _split3(x)
    return _dot(tril_b, hi) + _dot(tril_b, mid) + _dot(tril_b, lo)


def _ssd_body(z_ref, xbc_ref, dte_ref, dt_ref, prev_ref, h0_ref, cw_ref, cb_ref, dtbe_ref, aloge_ref,
              dtb_ref, alog_ref, de_ref, ng_ref, y_ref, hl_ref, xp_scr, ht_scr, *, rows_in):
    c = pl.program_id(1)
    pairs = D_INNER // LANES

    @pl.when(c == 0)
    def _():
        xp_scr[0:SSM_HALO, :] = prev_ref[...]
        for k in range(pairs):
            ht_scr[:, k * LANES:(k + 1) * LANES] = h0_ref[k * LANES:(k + 1) * LANES, :].T

    def padded(ref):
        v = ref[...]
        if rows_in == CHUNK:
            return v
        return jnp.concatenate([v, jnp.zeros((CHUNK - rows_in, v.shape[1]), F32)], axis=0)

    xp_scr[SSM_HALO:SSM_HALO + CHUNK, :] = padded(xbc_ref)
    first = SSM_HALO - (SSM_CONV_W - 1)
    conv = jnp.broadcast_to(cb_ref[...], (CHUNK, D_XBC))
    for j in range(SSM_CONV_W):
        conv = conv + cw_ref[j:j + 1, :] * xp_scr[first + j:first + j + CHUNK, :]
    xc = _silu(conv)
    xs = xc[:, :D_INNER]
    dt_e = _softplus(padded(dte_ref) + dtbe_ref[...])
    dt_h = _softplus(padded(dt_ref) + dtb_ref[...])
    if rows_in != CHUNK:
        live = lax.broadcasted_iota(jnp.int32, (CHUNK, 1), 0) < rows_in
        xs = jnp.where(live, xs, 0.0)
        dt_e = jnp.where(live, dt_e, 0.0)
        dt_h = jnp.where(live, dt_h, 0.0)

    r_i = lax.broadcasted_iota(jnp.int32, (CHUNK, CHUNK), 0)
    c_i = lax.broadcasted_iota(jnp.int32, (CHUNK, CHUNK), 1)
    causal = r_i >= c_i
    tril_b = jnp.where(causal, 1.0, 0.0).astype(BF16)
    cs_e = _cumsum_rows(tril_b, dt_e * (-jnp.exp(aloge_ref[...])))
    cs_h = _cumsum_rows(tril_b, dt_h * (-jnp.exp(alog_ref[...])))
    cs_t = cs_h.T

    xdt = xs * dt_e
    decay_in = jnp.exp(cs_e)
    xdt_b = xdt.astype(BF16)
    xde_b = (xdt * jnp.exp(cs_e[CHUNK - 1:CHUNK, :] - cs_e)).astype(BF16)
    ht = ht_scr[...]
    ht_b = ht.astype(BF16)
    lane_lo = lax.broadcasted_iota(jnp.int32, (CHUNK, LANES), 1) < P_C
    ys = []
    for g in range(G_C):
        bm = xc[:, D_INNER + g * N_C:D_INNER + (g + 1) * N_C]
        cm = xc[:, D_INNER + (G_C + g) * N_C:D_INNER + (G_C + g + 1) * N_C]
        bm_b = bm.astype(BF16)
        cm_b = cm.astype(BF16)
        cb = _dot_nt(cm_b, bm_b)
        bm_t = bm.T.astype(BF16)
        for k in range(pairs // G_C):
            kk = g * (pairs // G_C) + k
            cols = slice(kk * LANES, (kk + 1) * LANES)
            mats = []
            for hh in (2 * kk, 2 * kk + 1):
                seg = cs_h[:, hh:hh + 1] - cs_t[hh:hh + 1, :]
                mats.append((cb * jnp.exp(jnp.where(causal, seg, NEG_INF))).astype(BF16))
            xp2 = xdt_b[:, cols]
            y_diag = jnp.where(lane_lo, _dot(mats[0], xp2), _dot(mats[1], xp2))
            y_off = _dot(cm_b, ht_b[:, cols]) * decay_in[:, cols]
            ht_scr[:, cols] = ht[:, cols] * decay_in[CHUNK - 1:CHUNK, cols] + _dot(bm_t, xde_b[:, cols])
            ys.append(y_diag + y_off + de_ref[:, cols] * xs[:, cols])
    z = padded(z_ref)
    half = D_INNER // G_C
    outs = []
    for g in range(G_C):
        yg = jnp.concatenate(ys[g * (pairs // G_C):(g + 1) * (pairs // G_C)], axis=1)
        yg = yg * _silu(z[:, g * half:(g + 1) * half])
        outs.append(_rms(yg, ng_ref[:, g * half:(g + 1) * half], 1e-5))
    y_ref[...] = jnp.concatenate(outs, axis=1)[:rows_in]
    xp_scr[0:SSM_HALO, :] = xp_scr[CHUNK:CHUNK + SSM_HALO, :]

    @pl.when(c == pl.num_programs(1) - 1)
    def _():
        for k in range(pairs):
            hl_ref[k * LANES:(k + 1) * LANES, :] = ht_scr[:, k * LANES:(k + 1) * LANES].T


def _ssd(z, xbc, dte, dt, prev, h0, cw, cb, dtbe, aloge, dtb, alog, de, ng, nb, rows_in):
    t = z.shape[0]
    nc = t // nb // rows_in
    body = functools.partial(_ssd_body, rows_in=rows_in)
    row = lambda b, c: (b * nc + c, 0)
    full = lambda b, c: (0, 0)
    seq = lambda b, c: (b, 0, 0)
    vec = lambda a: pl.BlockSpec(a.shape, full)
    return pl.pallas_call(
        body,
        grid=(nb, nc),
        in_specs=[pl.BlockSpec((rows_in, D_INNER), row), pl.BlockSpec((rows_in, D_XBC), row),
                  pl.BlockSpec((rows_in, D_INNER), row), pl.BlockSpec((rows_in, LANES), row),
                  pl.BlockSpec((None, SSM_HALO, D_XBC), seq), pl.BlockSpec((None, D_INNER, N_C), seq),
                  vec(cw), vec(cb), vec(dtbe), vec(aloge), vec(dtb), vec(alog), vec(de), vec(ng)],
        out_specs=[pl.BlockSpec((rows_in, D_INNER), row), pl.BlockSpec((None, D_INNER, N_C), seq)],
        out_shape=[jax.ShapeDtypeStruct((t, D_INNER), F32), jax.ShapeDtypeStruct((nb, D_INNER, N_C), F32)],
        scratch_shapes=[pltpu.VMEM((SSM_HALO + CHUNK, D_XBC), F32), pltpu.VMEM((N_C, D_INNER), F32)],
        compiler_params=_params("parallel", "arbitrary"),
        name="ssd",
    )(z, xbc, dte, dt, prev, h0, cw, cb, dtbe, aloge, dtb, alog, de, ng)


def _gmlp_body(u_ref, v_ref, g_ref, beta_ref, ws_ref, bs_ref, d_ref, vn_ref):
    gu = _gelu_tanh(u_ref[...])
    gv = _gelu_tanh(v_ref[...])
    mu = jnp.mean(gv, axis=-1, keepdims=True)
    cen = gv - mu
    var = jnp.mean(cen * cen, axis=-1, keepdims=True)
    vn = cen * lax.rsqrt(var + 1e-5) * g_ref[...] + beta_ref[...]
    vn_ref[...] = vn
    r_i = lax.broadcasted_iota(jnp.int32, (CHUNK, CHUNK), 0)
    c_i = lax.broadcasted_iota(jnp.int32, (CHUNK, CHUNK), 1)
    dg = D_GMLP // G_D
    for g in range(G_D):
        wm = jnp.where(r_i >= c_i, ws_ref[g], 0.0).astype(BF16)
        s = _dot(wm, vn[:, g * dg:(g + 1) * dg].astype(BF16)) + bs_ref[:, g:g + 1]
        d_ref[:, g * dg:(g + 1) * dg] = gu[:, g * dg:(g + 1) * dg] * s


def _gmlp(u, v, g, beta, ws, bs_t, keep_all_vn):
    t = u.shape[0]
    row = lambda i: (i, 0)
    full = lambda i: (0, 0)
    if keep_all_vn:
        vn_spec, vn_rows = pl.BlockSpec((CHUNK, D_GMLP), row), t
    else:
        vn_spec, vn_rows = pl.BlockSpec((CHUNK, D_GMLP), full), CHUNK
    return pl.pallas_call(
        _gmlp_body,
        grid=(t // CHUNK,),
        in_specs=[pl.BlockSpec((CHUNK, D_GMLP), row), pl.BlockSpec((CHUNK, D_GMLP), row),
                  pl.BlockSpec((1, D_GMLP), full), pl.BlockSpec((1, D_GMLP), full),
                  pl.BlockSpec(ws.shape, lambda i: (0, 0, 0)), pl.BlockSpec(bs_t.shape, full)],
        out_specs=[pl.BlockSpec((CHUNK, D_GMLP), row), vn_spec],
        out_shape=[jax.ShapeDtypeStruct((t, D_GMLP), F32), jax.ShapeDtypeStruct((vn_rows, D_GMLP), F32)],
        compiler_params=_params("arbitrary"),
        name="gmlp",
    )(u, v, g, beta, ws, bs_t)


def _router_body(y_ref, g_ref, wr_ref, br_ref, xn_ref, comb_ref):
    hn = _rms(y_ref[...], g_ref[...], 1e-6)
    xn_ref[...] = hn.astype(BF16)
    logits = jnp.dot(hn, wr_ref[...], preferred_element_type=F32, precision=lax.Precision.HIGHEST) + br_ref[...]
    lane = lax.broadcasted_iota(jnp.int32, logits.shape, 1)
    lane_f = lane.astype(F32)
    far = float(LANES)
    is_g = (lane >= N_EXPERTS) & (lane < N_EXPERTS + E_GROUPS)
    gl = jnp.where(is_g, logits, -jnp.inf)
    gmax = jnp.max(gl, axis=1, keepdims=True)
    g_lane = jnp.min(jnp.where(gl == gmax, lane_f, far), axis=1, keepdims=True)
    g_w = 1.0 / jnp.sum(jnp.where(is_g, jnp.exp(gl - gmax), 0.0), axis=1, keepdims=True)
    g_sel = g_lane.astype(jnp.int32) - N_EXPERTS
    in_grp = (lane < N_EXPERTS) & ((lane // E_PER_GROUP) == g_sel)
    el = jnp.where(in_grp, logits, -jnp.inf)
    v1 = jnp.max(el, axis=1, keepdims=True)
    i1 = jnp.min(jnp.where(el == v1, lane_f, far), axis=1, keepdims=True)
    el2 = jnp.where(lane_f == i1, -jnp.inf, el)
    v2 = jnp.max(el2, axis=1, keepdims=True)
    i2 = jnp.min(jnp.where(el2 == v2, lane_f, far), axis=1, keepdims=True)
    r = jnp.exp(v2 - v1)
    w1 = 1.0 / (1.0 + r)
    w2 = r * w1
    comb_ref[...] = g_w * (jnp.where(lane_f == i1, w1, 0.0) + jnp.where(lane_f == i2, w2, 0.0))


def _router(y, g, wr, br):
    t = y.shape[0]
    tm = _row_tile(t, 512)
    row = lambda i: (i, 0)
    full = lambda i: (0, 0)
    return pl.pallas_call(
        _router_body,
        grid=(t // tm,),
        in_specs=[pl.BlockSpec((tm, D_MODEL), row), pl.BlockSpec((1, D_MODEL), full),
                  pl.BlockSpec(wr.shape, full), pl.BlockSpec((1, LANES), full)],
        out_specs=[pl.BlockSpec((tm, D_MODEL), row), pl.BlockSpec((tm, LANES), row)],
        out_shape=[jax.ShapeDtypeStruct((t, D_MODEL), BF16), jax.ShapeDtypeStruct((t, LANES), F32)],
        compiler_params=_params("parallel"),
        name="router",
    )(y, g, wr, br)


def _moe_dense_body(x_ref, c_ref, res_ref, wg_ref, wu_ref, wd_ref, o_ref):
    e = pl.program_id(1)

    @pl.when(e == 0)
    def _():
        o_ref[...] = res_ref[...]

    x = x_ref[...]
    hg = _dot(x, wg_ref[...])
    hid = (_silu(hg) * _dot(x, wu_ref[...])).astype(BF16)
    lane = lax.broadcasted_iota(jnp.int32, c_ref.shape, 1)
    ce = jnp.sum(jnp.where(lane == e, c_ref[...], 0.0), axis=1, keepdims=True)
    o_ref[...] += ce * _dot(hid, wd_ref[...])


def _moe_dense(xn, comb, res, wg, wu, wd):
    t = xn.shape[0]
    tm = _row_tile(t, 2048)
    row = lambda i, e: (i, 0)
    return pl.pallas_call(
        _moe_dense_body,
        grid=(t // tm, N_EXPERTS),
        in_specs=[pl.BlockSpec((tm, D_MODEL), row), pl.BlockSpec((tm, LANES), row),
                  pl.BlockSpec((tm, D_MODEL), row),
                  pl.BlockSpec((None, D_MODEL, D_EXPERT), lambda i, e: (e, 0, 0)),
                  pl.BlockSpec((None, D_MODEL, D_EXPERT), lambda i, e: (e, 0, 0)),
                  pl.BlockSpec((None, D_EXPERT, D_MODEL), lambda i, e: (e, 0, 0))],
        out_specs=pl.BlockSpec((tm, D_MODEL), row),
        out_shape=jax.ShapeDtypeStruct((t, D_MODEL), F32),
        compiler_params=_params("parallel", "arbitrary"),
        name="moe_dense",
    )(xn, comb, res, wg, wu, wd)


def _final_norm_body(x_ref, g_ref, o_ref):
    o_ref[...] = _rms(x_ref[...], g_ref[...], 1e-6)


def _final_norm(x, g):
    t = x.shape[0]
    tm = _row_tile(t, 1024)
    return pl.pallas_call(
        _final_norm_body,
        grid=(t // tm,),
        in_specs=[pl.BlockSpec((tm, D_MODEL), lambda i: (i, 0)), pl.BlockSpec((1, D_MODEL), lambda i: (0, 0))],
        out_specs=pl.BlockSpec((tm, D_MODEL), lambda i: (i, 0)),
        out_shape=jax.ShapeDtypeStruct((t, D_MODEL), F32),
        compiler_params=_params("parallel"),
        name="final_norm",
    )(x, g)


def _row(v):
    return v.reshape(1, -1).astype(F32)


def _pad_lanes(v, n=LANES):
    return jnp.pad(v, [(0, 0)] * (v.ndim - 1) + [(0, n - v.shape[-1])])


def kernel(x_prompt, x_sample, cache_attn_k, cache_attn_v, state_conf_conv, state_ssm_conv, state_ssm, page_table,
           norm_mix_g, norm_moe_g, norm_final_g,
           even_w_in, attn_lam_q1, attn_lam_k1, attn_lam_q2, attn_lam_k2, attn_subln_g,
           conf_conv_w, conf_conv_b, conf_ln_g, conf_ln_b, even_w_out,
           odd_w_in, ssm_conv_w, ssm_conv_b, ssm_dt_bias, ssm_a_log, ssm_d, ssm_norm_g,
           gmlp_ln_g, gmlp_ln_b, gmlp_ws, gmlp_bs, odd_w_out,
           moe_rg_w, moe_rg_b, moe_re_w, moe_re_b, moe_w_gate, moe_w_up, moe_w_down):
    bp, lp, _ = x_prompt.shape
    db, ls, _ = x_sample.shape
    assert bp == 1 and lp % CHUNK == 0 and (db * ls) % CHUNK == 0 and CHUNK % ls == 0
    depth = norm_mix_g.shape[0]
    page = cache_attn_k.shape[2]
    yp = x_prompt.reshape(bp * lp, D_MODEL)
    ys = x_sample.reshape(db * ls, D_MODEL)
    outs_p = {k: [] for k in ("k", "v", "conf", "sconv", "ssm", "gmlp")}
    outs_s = {k: [] for k in ("k", "v", "conf", "sconv", "ssm", "gmlp")}

    for layer in range(depth):
        g_mix = _row(norm_mix_g[layer])
        if layer % 2 == 0:
            e = layer // 2
            lambda_init = 0.8 - 0.6 * math.exp(-0.3 * layer)
            w_in = _split_weight(even_w_in[e].astype(F32))
            w_out = (_split_weight(even_w_out[e, :D_ATT].astype(F32))
                     + _split_weight(even_w_out[e, D_ATT:].astype(F32)))
            lamv = _pad_lanes(jnp.stack([attn_lam_q1[e], attn_lam_k1[e], attn_lam_q2[e], attn_lam_k2[e]]).astype(F32))
            lamv = jnp.pad(lamv, [(0, 4), (0, 0)])
            subln = _row(attn_subln_g[e])
            cw = jnp.pad(conf_conv_w[e].astype(F32), [(0, CONF_HALO - CONF_W), (0, 0)])
            cb, lg, lb = _row(conf_conv_b[e]), _row(conf_ln_g[e]), _row(conf_ln_b[e])

            qx, kx, k, v, vx, glu = _even_in(yp, g_mix, *w_in)
            attn = _attn_prompt(lamv, qx, kx, vx, subln, lambda_init)
            conv = _conf_conv(glu, glu, cw, cb, lg, lb, _row_tile(lp, 256), True)
            yp = _out_proj(attn, conv, yp, *w_out)
            outs_p["k"].append(k.reshape(bp, lp, H_A, HEAD_W))
            outs_p["v"].append(v.reshape(bp, lp, H_A, HEAD_W))
            outs_p["conf"].append(glu[lp - (CONF_W - 1):].reshape(bp, CONF_W - 1, D_CONF))

            qx, kx, k, v, _, glu = _even_in(ys, g_mix, *w_in)
            qparts = qx.astype(F32).reshape(2, db * ls, H_A, 4, DK_A)
            zero = jnp.zeros((db * ls, H_A, DK_A), F32)
            q_hi = jnp.stack([jnp.stack([qparts[0, :, :, 0], zero], axis=2),
                              jnp.stack([zero, qparts[1, :, :, 0]], axis=2)]).reshape(2, db * ls, D_ATT)
            q_lo = jnp.stack([jnp.stack([qparts[0, :, :, 2], zero], axis=2),
                              jnp.stack([zero, qparts[1, :, :, 2]], axis=2)]).reshape(2, db * ls, D_ATT)
            n_pool = cache_attn_k.shape[1]
            attn = _attn_sample(page_table + e * n_pool, lamv, q_hi, q_lo, k, v, subln,
                                cache_attn_k.reshape(-1, page, D_ATT), cache_attn_v.reshape(-1, page, D_ATT),
                                ls, lambda_init)
            hist = state_conf_conv[e].astype(F32)
            halo = jnp.pad(hist, [(0, 0), (CONF_HALO - (CONF_W - 1), 0), (0, 0)])
            conv = _conf_conv(glu, halo, cw, cb, lg, lb, ls, False)
            ys = _out_proj(attn, conv, ys, *w_out)
            outs_s["k"].append(k.reshape(db, ls, H_A, HEAD_W))
            outs_s["v"].append(v.reshape(db, ls, H_A, HEAD_W))
            outs_s["conf"].append(jnp.concatenate([hist, glu.reshape(db, ls, D_CONF)], axis=1)[:, ls:])
        else:
            o = layer // 2
            w = odd_w_in[o]
            i1, i2, i3, i4 = D_INNER, D_INNER + D_XBC, D_INNER + D_XBC + H_C, D_INNER + D_XBC + H_C + D_GMLP
            w_dt = w[:, i2:i3]
            w_in = jnp.concatenate([w[:, :i1], w[:, i1:i2], w[:, i3:i4], w[:, i4:],
                                    jnp.repeat(w_dt, P_C, axis=1), _pad_lanes(w_dt)], axis=1).astype(BF16)
            wy = odd_w_out[o, :D_INNER].astype(BF16)
            wd_ = odd_w_out[o, D_INNER:].astype(BF16)
            cw = jnp.pad(ssm_conv_w[o].astype(F32), [(0, SSM_HALO - SSM_CONV_W), (0, 0)])
            cb = _row(ssm_conv_b[o])
            dtb, alog = _pad_lanes(_row(ssm_dt_bias[o])), _pad_lanes(_row(ssm_a_log[o]))
            dtbe, aloge = _row(jnp.repeat(ssm_dt_bias[o], P_C)), _row(jnp.repeat(ssm_a_log[o], P_C))
            de = _row(jnp.repeat(ssm_d[o], P_C))
            ng = _row(ssm_norm_g[o])
            lg, lb = _row(gmlp_ln_g[o]), _row(gmlp_ln_b[o])
            ws = gmlp_ws[o].astype(F32)
            bs_t = _pad_lanes(gmlp_bs[o].astype(F32).T)

            z, xbc, u, v, dte, dt = _odd_in(yp, g_mix, w_in)
            prev0 = jnp.zeros((bp, SSM_HALO, D_XBC), F32)
            h00 = jnp.zeros((bp, D_INNER, N_C), F32)
            y, h_new = _ssd(z, xbc, dte, dt, prev0, h00, cw, cb, dtbe, aloge, dtb, alog, de, ng, bp, CHUNK)
            d_out, vn = _gmlp(u, v, lg, lb, ws, bs_t, False)
            yp = _out_proj(y, d_out, yp, wy, wd_)
            outs_p["sconv"].append(xbc[lp - (SSM_CONV_W - 1):].reshape(bp, SSM_CONV_W - 1, D_XBC))
            outs_p["ssm"].append(h_new.reshape(bp, H_C, P_C, N_C))
            outs_p["gmlp"].append(vn.reshape(bp, CHUNK, D_GMLP))

            z, xbc, u, v, dte, dt = _odd_in(ys, g_mix, w_in)
            hist = state_ssm_conv[o].astype(F32)
            prev = jnp.pad(hist, [(0, 0), (SSM_HALO - (SSM_CONV_W - 1), 0), (0, 0)])
            y, h_new = _ssd(z, xbc, dte, dt, prev, state_ssm[o].astype(F32).reshape(db, D_INNER, N_C),
                            cw, cb, dtbe, aloge, dtb, alog, de, ng, db, ls)
            eye = jnp.eye(CHUNK // ls, dtype=F32)
            ws_s = jnp.stack([jnp.kron(eye, ws[g, :ls, :ls]) for g in range(G_D)])
            bs_s = jnp.tile(bs_t[:ls], (CHUNK // ls, 1))
            d_out, vn = _gmlp(u, v, lg, lb, ws_s, bs_s, True)
            ys = _out_proj(y, d_out, ys, wy, wd_)
            outs_s["sconv"].append(jnp.concatenate([hist, xbc.reshape(db, ls, D_XBC)], axis=1)[:, ls:])
            outs_s["ssm"].append(h_new.reshape(db, H_C, P_C, N_C))
            outs_s["gmlp"].append(vn.reshape(db, ls, D_GMLP))

        g_moe = _row(norm_moe_g[layer])
        wr = _pad_lanes(jnp.concatenate([moe_re_w[layer], moe_rg_w[layer]], axis=1).astype(F32))
        br = _pad_lanes(_row(jnp.concatenate([moe_re_b[layer], moe_rg_b[layer]])))
        wg = moe_w_gate[layer].astype(BF16)
        wu = moe_w_up[layer].astype(BF16)
        wd = moe_w_down[layer].astype(BF16)
        xn, comb = _router(yp, g_moe, wr, br)
        yp = _moe_dense(xn, comb, yp, wg, wu, wd)
        xn, comb = _router(ys, g_moe, wr, br)
        ys = _moe_dense(xn, comb, ys, wg, wu, wd)

    g_fin = _row(norm_final_g)
    y_prompt = _final_norm(yp, g_fin).reshape(bp, lp, D_MODEL)
    y_sample = _final_norm(ys, g_fin).reshape(db, ls, D_MODEL)
    st = lambda xs: jnp.stack(xs)
    return (y_prompt, y_sample,
            st(outs_p["k"]), st(outs_p["v"]), st(outs_p["conf"]), st(outs_p["sconv"]), st(outs_p["ssm"]),
            st(outs_p["gmlp"]),
            st(outs_s["k"]), st(outs_s["v"]), st(outs_s["conf"]), st(outs_s["sconv"]), st(outs_s["ssm"]),
            st(outs_s["gmlp"]))
```

```python
import functools
import math

import jax
import jax.numpy as jnp
from jax import lax
from jax.experimental import pallas as pl
from jax.experimental.pallas import tpu as pltpu

F32 = jnp.float32
BF16 = jnp.bfloat16

D_MODEL = 1024
H_A = 4
DK_A = 64
HEAD_W = 128
D_ATT = H_A * HEAD_W
D_CONF = 512
CONF_W = 31
CONF_HALO = 32
H_C = 16
P_C = 64
D_INNER = H_C * P_C
G_C = 2
N_C = 128
D_XBC = D_INNER + 2 * G_C * N_C
SSM_CONV_W = 4
SSM_HALO = 8
CHUNK = 128
G_D = 4
D_GMLP = 512
E_GROUPS = 4
E_PER_GROUP = 8
N_EXPERTS = 32
D_EXPERT = 256
NEG_INF = -1e30
LOG2_E = math.log2(math.e)
LANES = 128
VMEM_LIMIT = 56 * 1024 * 1024


def _params(*sem):
    return pltpu.CompilerParams(dimension_semantics=sem, vmem_limit_bytes=VMEM_LIMIT)


def _sigmoid(x):
    return 1.0 / (1.0 + jnp.exp(-x))


def _silu(x):
    return x * _sigmoid(x)


def _gelu_tanh(x):
    return 0.5 * x * (1.0 + jnp.tanh(math.sqrt(2.0 / math.pi) * (x + 0.044715 * (x * x * x))))


def _softplus(x):
    return jnp.maximum(x, 0.0) + jnp.log(1.0 + jnp.exp(-jnp.abs(x)))


def _rms(x, g, eps):
    return x * lax.rsqrt(jnp.mean(x * x, axis=-1, keepdims=True) + eps) * g


def _dot(a, b):
    return jnp.dot(a, b, preferred_element_type=F32)


def _dot_nt(a, b):
    return lax.dot_general(a, b, (((1,), (1,)), ((), ())), preferred_element_type=F32)


def _row_tile(t, cap):
    tm = min(t, cap)
    while t % tm:
        tm //= 2
    return tm


def _hi_f32(x):
    return x.astype(BF16).astype(F32)


def _split2(x):
    hi = x.astype(BF16)
    return hi, (x - hi.astype(F32)).astype(BF16)


def _dot3(a_hi, a_lo, w_hi, w_lo):
    return _dot(a_hi, w_hi) + _dot(a_hi, w_lo) + _dot(a_lo, w_hi)


def _split_weight_body(w_ref, hi_ref, lo_ref):
    hi, lo = _split2(w_ref[...])
    hi_ref[...] = hi
    lo_ref[...] = lo


def _split_weight(w):
    r, c = w.shape
    tr = _row_tile(r, 256)
    spec = pl.BlockSpec((tr, c), lambda i: (i, 0))
    return pl.pallas_call(
        _split_weight_body,
        grid=(r // tr,),
        in_specs=[spec],
        out_specs=[spec, spec],
        out_shape=[jax.ShapeDtypeStruct((r, c), BF16), jax.ShapeDtypeStruct((r, c), BF16)],
        compiler_params=_params("parallel"),
        name="split_weight",
    )(w)


def _even_in_body(x_ref, g_ref, wh_ref, wl_ref, qx_ref, kx_ref, k_ref, v_ref, vx_ref, glu_ref):
    h_hi, h_lo = _split2(_rms(x_ref[...], g_ref[...], 1e-6))

    def proj(c0, n):
        return _dot3(h_hi, h_lo, wh_ref[:, c0:c0 + n], wl_ref[:, c0:c0 + n])

    q = proj(0, D_ATT) * (DK_A ** -0.5 * LOG2_E)
    k = proj(D_ATT, D_ATT)
    k_ref[...] = k
    first = lax.broadcasted_iota(jnp.int32, (q.shape[0], HEAD_W), 1) < DK_A
    for hd in range(H_A):
        for src, dst, is_q in ((q, qx_ref, True), (k, kx_ref, False)):
            blk = src[:, hd * HEAD_W:(hd + 1) * HEAD_W]
            rot = pltpu.roll(blk, DK_A, 1)
            bh, rh = _hi_f32(blk), _hi_f32(rot)
            bl, rl = blk - bh, rot - rh
            lo_a = hd * 2 * HEAD_W
            lo_b = lo_a + HEAD_W
            if is_q:
                parts = ((jnp.where(first, bh, rh), jnp.where(first, bl, 0.0)),
                         (jnp.where(first, rh, bh), jnp.where(first, rl, 0.0)))
            else:
                parts = ((jnp.where(first, bh, rl), jnp.where(first, bh, 0.0)),
                         (jnp.where(first, rh, bl), jnp.where(first, rh, 0.0)))
            for i in range(2):
                dst[i, :, lo_a:lo_b] = parts[i][0].astype(BF16)
                dst[i, :, lo_b:lo_b + HEAD_W] = parts[i][1].astype(BF16)
    v = proj(2 * D_ATT, D_ATT)
    v_ref[...] = v
    v_hi, v_lo = _split2(v)
    for hd in range(H_A):
        vx_ref[:, 2 * hd * HEAD_W:(2 * hd + 1) * HEAD_W] = v_hi[:, hd * HEAD_W:(hd + 1) * HEAD_W]
        vx_ref[:, (2 * hd + 1) * HEAD_W:(2 * hd + 2) * HEAD_W] = v_lo[:, hd * HEAD_W:(hd + 1) * HEAD_W]
    a = proj(3 * D_ATT, D_CONF)
    gate = proj(3 * D_ATT + D_CONF, D_CONF)
    glu_ref[...] = a * _sigmoid(gate)


def _even_in(x, g, wh, wl):
    t = x.shape[0]
    tm = _row_tile(t, 256)
    row = lambda i: (i, 0)
    full = lambda i: (0, 0)
    wide = lambda i: (0, i, 0)
    return pl.pallas_call(
        _even_in_body,
        grid=(t // tm,),
        in_specs=[pl.BlockSpec((tm, D_MODEL), row), pl.BlockSpec((1, D_MODEL), full),
                  pl.BlockSpec(wh.shape, full), pl.BlockSpec(wl.shape, full)],
        out_specs=[pl.BlockSpec((2, tm, 2 * D_ATT), wide), pl.BlockSpec((2, tm, 2 * D_ATT), wide),
                   pl.BlockSpec((tm, D_ATT), row), pl.BlockSpec((tm, D_ATT), row),
                   pl.BlockSpec((tm, 2 * D_ATT), row), pl.BlockSpec((tm, D_CONF), row)],
        out_shape=[jax.ShapeDtypeStruct((2, t, 2 * D_ATT), BF16), jax.ShapeDtypeStruct((2, t, 2 * D_ATT), BF16),
                   jax.ShapeDtypeStruct((t, D_ATT), F32), jax.ShapeDtypeStruct((t, D_ATT), F32),
                   jax.ShapeDtypeStruct((t, 2 * D_ATT), BF16), jax.ShapeDtypeStruct((t, D_CONF), F32)],
        compiler_params=_params("parallel"),
        name="even_in",
    )(x, g, wh, wl)


def _lambda(lamv_ref, lambda_init):
    t1 = jnp.sum(lamv_ref[0:1, :] * lamv_ref[1:2, :], axis=1, keepdims=True)
    t2 = jnp.sum(lamv_ref[2:3, :] * lamv_ref[3:4, :], axis=1, keepdims=True)
    return jnp.exp(t1) - jnp.exp(t2) + lambda_init


def _online_update(s, v_tiles, m_scr, l_scr, acc_scr):
    rows = s.shape[0]
    m_prev = m_scr[...]
    m_new = jnp.maximum(m_prev, jnp.max(s, axis=1, keepdims=True))
    alpha = jnp.exp2(m_prev - m_new)
    p = jnp.exp2(s - m_new)
    l_scr[...] = alpha * l_scr[...] + jnp.sum(p, axis=1, keepdims=True)
    p_hi, p_lo = _split2(p)
    p_hl = jnp.concatenate([p_hi, p_lo], axis=0)
    pv = None
    for off, width, vt in v_tiles:
        v_hi, v_lo = _split2(vt)
        both = _dot(p_hl[:, off:off + width], v_hi)
        d = both[:rows] + both[rows:] + _dot(p_hi[:, off:off + width], v_lo)
        pv = d if pv is None else pv + d
    acc_scr[...] = alpha * acc_scr[...] + pv
    m_scr[...] = m_new


def _attn_prompt_body(lamv_ref, q_ref, k_ref, v_ref, g_ref, o_ref, m_scr, l_scr, acc_scr, *, tq, rc, lambda_init):
    i = pl.program_id(1)
    m_scr[...] = jnp.full(m_scr.shape, NEG_INF, F32)
    l_scr[...] = jnp.zeros(l_scr.shape, F32)
    acc_scr[...] = jnp.zeros(acc_scr.shape, F32)

    def tile(j, masked):
        start = pl.multiple_of(j * tq, tq)
        vx = v_ref[pl.ds(start, tq), :]
        for mp in range(2):
            kt = k_ref[mp, pl.ds(start, tq), :]
            for c in range(tq // rc):
                r0 = mp * tq + c * rc
                s = _dot_nt(q_ref[mp, c * rc:(c + 1) * rc, :], kt)
                if masked:
                    row = lax.broadcasted_iota(jnp.int32, s.shape, 0) + c * rc
                    col = lax.broadcasted_iota(jnp.int32, s.shape, 1)
                    s = jnp.where(col <= row, s, NEG_INF)
                tiles = [s[:, b * LANES:(b + 1) * LANES] for b in range(tq // LANES)]
                top = functools.reduce(jnp.maximum, tiles)
                m_prev = m_scr[r0:r0 + rc, :]
                m_new = jnp.maximum(m_prev, jnp.max(top, axis=1, keepdims=True))
                alpha = jnp.exp2(m_prev - m_new)
                ps = [jnp.exp2(t - m_new) for t in tiles]
                l_scr[r0:r0 + rc, :] = (alpha * l_scr[r0:r0 + rc, :]
                                        + jnp.sum(functools.reduce(jnp.add, ps), axis=1, keepdims=True))
                p_hi, p_lo = _split2(jnp.concatenate(ps, axis=1))
                pv = _dot(p_hi, vx)
                acc_scr[r0:r0 + rc, :] = (alpha * acc_scr[r0:r0 + rc, :] + pv[:, :HEAD_W] + pv[:, HEAD_W:]
                                          + _dot(p_lo, vx[:, :HEAD_W]))
                m_scr[r0:r0 + rc, :] = m_new

    def body(j, carry):
        tile(j, False)
        return carry

    lax.fori_loop(0, i, body, 0)
    tile(i, True)
    lam = _lambda(lamv_ref, lambda_init)
    o1 = acc_scr[0:tq, :] / l_scr[0:tq, :]
    o2 = acc_scr[tq:, :] / l_scr[tq:, :]
    o_ref[...] = _rms(o1 - lam * o2, g_ref[...], 1e-5) * (1.0 - lambda_init)


def _attn_prompt(lamv, qx, kx, vx, subln_g, lambda_init):
    t = vx.shape[0]
    tq = _row_tile(t, 512)
    rc = _row_tile(tq, 256)
    body = functools.partial(_attn_prompt_body, tq=tq, rc=rc, lambda_init=lambda_init)
    return pl.pallas_call(
        body,
        grid=(H_A, t // tq),
        in_specs=[pl.BlockSpec((8, LANES), lambda h, i: (0, 0)),
                  pl.BlockSpec((2, tq, 2 * HEAD_W), lambda h, i: (0, i, h)),
                  pl.BlockSpec((2, t, 2 * HEAD_W), lambda h, i: (0, 0, h)),
                  pl.BlockSpec((t, 2 * HEAD_W), lambda h, i: (0, h)),
                  pl.BlockSpec((1, HEAD_W), lambda h, i: (0, 0))],
        out_specs=pl.BlockSpec((tq, HEAD_W), lambda h, i: (i, h)),
        out_shape=jax.ShapeDtypeStruct((t, D_ATT), F32),
        scratch_shapes=[pltpu.VMEM((2 * tq, LANES), F32), pltpu.VMEM((2 * tq, LANES), F32),
                        pltpu.VMEM((2 * tq, HEAD_W), F32)],
        compiler_params=_params("parallel", "arbitrary"),
        name="attn_prompt",
    )(lamv, qx, kx, vx, subln_g)


def _attn_sample_body(pt_ref, lamv_ref, qh_ref, ql_ref, kn_ref, vn_ref, g_ref, *rest, n_pg, lq, lambda_init):
    k_refs = rest[:n_pg]
    v_refs = rest[n_pg:2 * n_pg]
    o_ref, m_scr, l_scr, acc_scr = rest[2 * n_pg:]
    j = pl.program_id(1)
    rows = 2 * H_A * lq
    pr = k_refs[0].shape[0]

    @pl.when(j == 0)
    def _():
        m_scr[...] = jnp.full(m_scr.shape, NEG_INF, F32)
        l_scr[...] = jnp.zeros(l_scr.shape, F32)
        acc_scr[...] = jnp.zeros(acc_scr.shape, F32)

    qhl = jnp.concatenate([qh_ref[...], ql_ref[...]], axis=0).astype(BF16)
    qh = qhl[:rows]
    row_i = lax.broadcasted_iota(jnp.int32, (rows, pr), 0)
    col_i = lax.broadcasted_iota(jnp.int32, (rows, pr), 1)
    same_head = ((row_i // lq) % H_A) == (col_i % H_A)

    def scores(k):
        kh, kl = _split2(k)
        both = _dot_nt(qhl, kh)
        return both[:rows] + both[rows:] + _dot_nt(qh, kl)

    s = jnp.concatenate([jnp.where(same_head, scores(k_refs[p][...]), NEG_INF) for p in range(n_pg)], axis=1)
    _online_update(s, [(p * pr, pr, v_refs[p][...]) for p in range(n_pg)], m_scr, l_scr, acc_scr)

    @pl.when(j == pl.num_programs(1) - 1)
    def _():
        pad = jnp.zeros((pr - lq * H_A, HEAD_W), F32)
        sn = scores(jnp.concatenate([kn_ref[...], pad], axis=0))
        causal = (col_i // H_A) <= (row_i % lq)
        sn = jnp.where(same_head & causal, sn, NEG_INF)
        _online_update(sn, [(0, pr, jnp.concatenate([vn_ref[...], pad], axis=0))], m_scr, l_scr, acc_scr)
        lam = _lambda(lamv_ref, lambda_init)
        o = acc_scr[...] / l_scr[...]
        d = o[:H_A * lq] - lam * o[H_A * lq:]
        y = _rms(d, g_ref[...], 1e-5) * (1.0 - lambda_init)
        o_ref[...] = jnp.concatenate([y[h * lq:(h + 1) * lq] for h in range(H_A)], axis=1)


def _attn_sample(page_table, lamv, q_hi, q_lo, k_new, v_new, subln_g, cache_k, cache_v, lq, lambda_init):
    nb, n_pages = page_table.shape
    pr = cache_k.shape[1]
    n_pg = math.gcd(n_pages, 8)
    rows = 2 * H_A * lq
    body = functools.partial(_attn_sample_body, n_pg=n_pg, lq=lq, lambda_init=lambda_init)

    def page_spec(p):
        return pl.BlockSpec((None, pr, HEAD_W), lambda b, j, pt, p=p: (pt[b, j * n_pg + p], 0, 0))

    per_seq = lambda b, j, pt: (b, 0)
    grid_spec = pltpu.PrefetchScalarGridSpec(
        num_scalar_prefetch=1,
        grid=(nb, n_pages // n_pg),
        in_specs=[pl.BlockSpec((8, LANES), lambda b, j, pt: (0, 0)),
                  pl.BlockSpec((rows, HEAD_W), per_seq), pl.BlockSpec((rows, HEAD_W), per_seq),
                  pl.BlockSpec((lq * H_A, HEAD_W), per_seq), pl.BlockSpec((lq * H_A, HEAD_W), per_seq),
                  pl.BlockSpec((1, HEAD_W), lambda b, j, pt: (0, 0))]
                 + [page_spec(p) for p in range(n_pg)] + [page_spec(p) for p in range(n_pg)],
        out_specs=pl.BlockSpec((lq, D_ATT), per_seq),
        scratch_shapes=[pltpu.VMEM((rows, 1), F32), pltpu.VMEM((rows, 1), F32),
                        pltpu.VMEM((rows, HEAD_W), F32)],
    )
    return pl.pallas_call(
        body,
        grid_spec=grid_spec,
        out_shape=jax.ShapeDtypeStruct((nb * lq, D_ATT), F32),
        compiler_params=_params("parallel", "arbitrary"),
        name="attn_sample",
    )(page_table, lamv, q_hi, q_lo, k_new, v_new, subln_g, *([cache_k] * n_pg), *([cache_v] * n_pg))


def _conf_conv_body(x_ref, halo_ref, w_ref, b_ref, g_ref, beta_ref, o_ref, xp_scr, *, tt, rb, halo_is_x):
    halo = halo_ref[...]
    if halo_is_x:
        halo = jnp.where(pl.program_id(0) == 0, 0.0, halo)
    xp_scr[0:CONF_HALO, :] = halo
    xp_scr[CONF_HALO:CONF_HALO + tt, :] = x_ref[...]
    first = CONF_HALO - (CONF_W - 1)
    for r0 in range(0, tt, rb):
        acc = jnp.broadcast_to(b_ref[...], (rb, D_CONF))
        for j in range(CONF_W):
            acc = acc + w_ref[j:j + 1, :] * xp_scr[r0 + first + j:r0 + first + j + rb, :]
        mu = jnp.mean(acc, axis=-1, keepdims=True)
        cen = acc - mu
        var = jnp.mean(cen * cen, axis=-1, keepdims=True)
        y = cen * lax.rsqrt(var + 1e-5) * g_ref[...] + beta_ref[...]
        o_ref[r0:r0 + rb, :] = _silu(y)


def _conf_conv(x, halo_src, w, b, g, beta, tt, halo_is_x):
    t = x.shape[0]
    rb = min(tt, 64)
    body = functools.partial(_conf_conv_body, tt=tt, rb=rb, halo_is_x=halo_is_x)
    if halo_is_x:
        per = tt // CONF_HALO
        halo_spec = pl.BlockSpec((CONF_HALO, D_CONF), lambda i: (jnp.maximum(i * per - 1, 0), 0))
    else:
        halo_spec = pl.BlockSpec((None, CONF_HALO, D_CONF), lambda i: (i, 0, 0))
    full = lambda i: (0, 0)
    return pl.pallas_call(
        body,
        grid=(t // tt,),
        in_specs=[pl.BlockSpec((tt, D_CONF), lambda i: (i, 0)), halo_spec,
                  pl.BlockSpec(w.shape, full), pl.BlockSpec((1, D_CONF), full),
                  pl.BlockSpec((1, D_CONF), full), pl.BlockSpec((1, D_CONF), full)],
        out_specs=pl.BlockSpec((tt, D_CONF), lambda i: (i, 0)),
        out_shape=jax.ShapeDtypeStruct((t, D_CONF), F32),
        scratch_shapes=[pltpu.VMEM((CONF_HALO + tt, D_CONF), F32)],
        compiler_params=_params("parallel"),
        name="conf_conv",
    )(x, halo_src, w, b, g, beta)


def _out_proj_body(a_ref, c_ref, res_ref, wa_ref, wc_ref, o_ref):
    o_ref[...] = (res_ref[...] + _dot(a_ref[...].astype(BF16), wa_ref[...])
                  + _dot(c_ref[...].astype(BF16), wc_ref[...]))


def _out_proj_split_body(a_ref, c_ref, res_ref, wah_ref, wal_ref, wch_ref, wcl_ref, o_ref):
    a_hi, a_lo = _split2(a_ref[...])
    c_hi, c_lo = _split2(c_ref[...])
    o_ref[...] = (res_ref[...] + _dot3(a_hi, a_lo, wah_ref[...], wal_ref[...])
                  + _dot3(c_hi, c_lo, wch_ref[...], wcl_ref[...]))


def _out_proj(a, c, res, *weights):
    t = a.shape[0]
    tm = _row_tile(t, 512)
    row = lambda i: (i, 0)
    full = lambda i: (0, 0)
    body = _out_proj_body if len(weights) == 2 else _out_proj_split_body
    return pl.pallas_call(
        body,
        grid=(t // tm,),
        in_specs=[pl.BlockSpec((tm, a.shape[1]), row), pl.BlockSpec((tm, c.shape[1]), row),
                  pl.BlockSpec((tm, D_MODEL), row)] + [pl.BlockSpec(w.shape, full) for w in weights],
        out_specs=pl.BlockSpec((tm, D_MODEL), row),
        out_shape=jax.ShapeDtypeStruct((t, D_MODEL), F32),
        compiler_params=_params("parallel"),
        name="out_proj",
    )(a, c, res, *weights)


_O_Z = 0
_O_XBC = _O_Z + D_INNER
_O_U = _O_XBC + D_XBC
_O_V = _O_U + D_GMLP
_O_DTE = _O_V + D_GMLP
_O_DT = _O_DTE + D_INNER
_O_END = _O_DT + LANES


def _odd_in_body(x_ref, g_ref, w_ref, z_ref, xbc_ref, u_ref, v_ref, dte_ref, dt_ref):
    h = _rms(x_ref[...], g_ref[...], 1e-6).astype(BF16)
    z_ref[...] = _dot(h, w_ref[:, _O_Z:_O_XBC])
    xbc_ref[...] = _dot(h, w_ref[:, _O_XBC:_O_U])
    u_ref[...] = _dot(h, w_ref[:, _O_U:_O_V])
    v_ref[...] = _dot(h, w_ref[:, _O_V:_O_DTE])
    dte_ref[...] = _dot(h, w_ref[:, _O_DTE:_O_DT])
    dt_ref[...] = _dot(h, w_ref[:, _O_DT:_O_END])


def _odd_in(x, g, w):
    t = x.shape[0]
    tm = _row_tile(t, 512)
    row = lambda i: (i, 0)
    full = lambda i: (0, 0)
    widths = (D_INNER, D_XBC, D_GMLP, D_GMLP, D_INNER, LANES)
    return pl.pallas_call(
        _odd_in_body,
        grid=(t // tm,),
        in_specs=[pl.BlockSpec((tm, D_MODEL), row), pl.BlockSpec((1, D_MODEL), full),
                  pl.BlockSpec(w.shape, full)],
        out_specs=[pl.BlockSpec((tm, n), row) for n in widths],
        out_shape=[jax.ShapeDtypeStruct((t, n), F32) for n in widths],
        compiler_params=_params("parallel"),
        name="odd_in",
    )(x, g, w)


def _split3(x):
    hi = x.astype(BF16)
    r1 = x - hi.astype(F32)
    mid = r1.astype(BF16)
    lo = (r1 - mid.astype(F32)).astype(BF16)
    return hi, mid, lo


def _cumsum_rows(tril_b, x):
    hi, mid, lo = _split3(x)
    return _dot(tril_b, hi) + _dot(tril_b, mid) + _dot(tril_b, lo)


def _ssd_body(z_ref, xbc_ref, dte_ref, dt_ref, prev_ref, h0_ref, cw_ref, cb_ref, dtbe_ref, aloge_ref,
              dtb_ref, alog_ref, de_ref, ng_ref, y_ref, hl_ref, xp_scr, ht_scr, *, rows_in):
    c = pl.program_id(1)
    pairs = D_INNER // LANES

    @pl.when(c == 0)
    def _():
        xp_scr[0:SSM_HALO, :] = prev_ref[...]
        for k in range(pairs):
            ht_scr[:, k * LANES:(k + 1) * LANES] = h0_ref[k * LANES:(k + 1) * LANES, :].T

    def padded(ref):
        v = ref[...]
        if rows_in == CHUNK:
            return v
        return jnp.concatenate([v, jnp.zeros((CHUNK - rows_in, v.shape[1]), F32)], axis=0)

    xp_scr[SSM_HALO:SSM_HALO + CHUNK, :] = padded(xbc_ref)
    first = SSM_HALO - (SSM_CONV_W - 1)
    conv = jnp.broadcast_to(cb_ref[...], (CHUNK, D_XBC))
    for j in range(SSM_CONV_W):
        conv = conv + cw_ref[j:j + 1, :] * xp_scr[first + j:first + j + CHUNK, :]
    xc = _silu(conv)
    xs = xc[:, :D_INNER]
    dt_e = _softplus(padded(dte_ref) + dtbe_ref[...])
    dt_h = _softplus(padded(dt_ref) + dtb_ref[...])
    if rows_in != CHUNK:
        live = lax.broadcasted_iota(jnp.int32, (CHUNK, 1), 0) < rows_in
        xs = jnp.where(live, xs, 0.0)
        dt_e = jnp.where(live, dt_e, 0.0)
        dt_h = jnp.where(live, dt_h, 0.0)

    r_i = lax.broadcasted_iota(jnp.int32, (CHUNK, CHUNK), 0)
    c_i = lax.broadcasted_iota(jnp.int32, (CHUNK, CHUNK), 1)
    causal = r_i >= c_i
    tril_b = jnp.where(causal, 1.0, 0.0).astype(BF16)
    cs_e = _cumsum_rows(tril_b, dt_e * (-jnp.exp(aloge_ref[...])))
    cs_h = _cumsum_rows(tril_b, dt_h * (-jnp.exp(alog_ref[...])))
    cs_t = cs_h.T

    xdt = xs * dt_e
    decay_in = jnp.exp(cs_e)
    xdt_b = xdt.astype(BF16)
    xde_b = (xdt * jnp.exp(cs_e[CHUNK - 1:CHUNK, :] - cs_e)).astype(BF16)
    ht = ht_scr[...]
    ht_b = ht.astype(BF16)
    lane_lo = lax.broadcasted_iota(jnp.int32, (CHUNK, LANES), 1) < P_C
    ys = []
    for g in range(G_C):
        bm = xc[:, D_INNER + g * N_C:D_INNER + (g + 1) * N_C]
        cm = xc[:, D_INNER + (G_C + g) * N_C:D_INNER + (G_C + g + 1) * N_C]
        bm_b = bm.astype(BF16)
        cm_b = cm.astype(BF16)
        cb = _dot_nt(cm_b, bm_b)
        bm_t = bm.T.astype(BF16)
        for k in range(pairs // G_C):
            kk = g * (pairs // G_C) + k
            cols = slice(kk * LANES, (kk + 1) * LANES)
            mats = []
            for hh in (2 * kk, 2 * kk + 1):
                seg = cs_h[:, hh:hh + 1] - cs_t[hh:hh + 1, :]
                mats.append((cb * jnp.exp(jnp.where(causal, seg, NEG_INF))).astype(BF16))
            xp2 = xdt_b[:, cols]
            y_diag = jnp.where(lane_lo, _dot(mats[0], xp2), _dot(mats[1], xp2))
            y_off = _dot(cm_b, ht_b[:, cols]) * decay_in[:, cols]
            ht_scr[:, cols] = ht[:, cols] * decay_in[CHUNK - 1:CHUNK, cols] + _dot(bm_t, xde_b[:, cols])
            ys.append(y_diag + y_off + de_ref[:, cols] * xs[:, cols])
    z = padded(z_ref)
    half = D_INNER // G_C
    outs = []
    for g in range(G_C):
        yg = jnp.concatenate(ys[g * (pairs // G_C):(g + 1) * (pairs // G_C)], axis=1)
        yg = yg * _silu(z[:, g * half:(g + 1) * half])
        outs.append(_rms(yg, ng_ref[:, g * half:(g + 1) * half], 1e-5))
    y_ref[...] = jnp.concatenate(outs, axis=1)[:rows_in]
    xp_scr[0:SSM_HALO, :] = xp_scr[CHUNK:CHUNK + SSM_HALO, :]

    @pl.when(c == pl.num_programs(1) - 1)
    def _():
        for k in range(pairs):
            hl_ref[k * LANES:(k + 1) * LANES, :] = ht_scr[:, k * LANES:(k + 1) * LANES].T


def _ssd(z, xbc, dte, dt, prev, h0, cw, cb, dtbe, aloge, dtb, alog, de, ng, nb, rows_in):
    t = z.shape[0]
    nc = t // nb // rows_in
    body = functools.partial(_ssd_body, rows_in=rows_in)
    row = lambda b, c: (b * nc + c, 0)
    full = lambda b, c: (0, 0)
    seq = lambda b, c: (b, 0, 0)
    vec = lambda a: pl.BlockSpec(a.shape, full)
    return pl.pallas_call(
        body,
        grid=(nb, nc),
        in_specs=[pl.BlockSpec((rows_in, D_INNER), row), pl.BlockSpec((rows_in, D_XBC), row),
                  pl.BlockSpec((rows_in, D_INNER), row), pl.BlockSpec((rows_in, LANES), row),
                  pl.BlockSpec((None, SSM_HALO, D_XBC), seq), pl.BlockSpec((None, D_INNER, N_C), seq),
                  vec(cw), vec(cb), vec(dtbe), vec(aloge), vec(dtb), vec(alog), vec(de), vec(ng)],
        out_specs=[pl.BlockSpec((rows_in, D_INNER), row), pl.BlockSpec((None, D_INNER, N_C), seq)],
        out_shape=[jax.ShapeDtypeStruct((t, D_INNER), F32), jax.ShapeDtypeStruct((nb, D_INNER, N_C), F32)],
        scratch_shapes=[pltpu.VMEM((SSM_HALO + CHUNK, D_XBC), F32), pltpu.VMEM((N_C, D_INNER), F32)],
        compiler_params=_params("parallel", "arbitrary"),
        name="ssd",
    )(z, xbc, dte, dt, prev, h0, cw, cb, dtbe, aloge, dtb, alog, de, ng)


def _gmlp_body(u_ref, v_ref, g_ref, beta_ref, ws_ref, bs_ref, d_ref, vn_ref):
    gu = _gelu_tanh(u_ref[...])
    gv = _gelu_tanh(v_ref[...])
    mu = jnp.mean(gv, axis=-1, keepdims=True)
    cen = gv - mu
    var = jnp.mean(cen * cen, axis=-1, keepdims=True)
    vn = cen * lax.rsqrt(var + 1e-5) * g_ref[...] + beta_ref[...]
    vn_ref[...] = vn
    r_i = lax.broadcasted_iota(jnp.int32, (CHUNK, CHUNK), 0)
    c_i = lax.broadcasted_iota(jnp.int32, (CHUNK, CHUNK), 1)
    dg = D_GMLP // G_D
    for g in range(G_D):
        wm = jnp.where(r_i >= c_i, ws_ref[g], 0.0).astype(BF16)
        s = _dot(wm, vn[:, g * dg:(g + 1) * dg].astype(BF16)) + bs_ref[:, g:g + 1]
        d_ref[:, g * dg:(g + 1) * dg] = gu[:, g * dg:(g + 1) * dg] * s


def _gmlp(u, v, g, beta, ws, bs_t, keep_all_vn):
    t = u.shape[0]
    row = lambda i: (i, 0)
    full = lambda i: (0, 0)
    if keep_all_vn:
        vn_spec, vn_rows = pl.BlockSpec((CHUNK, D_GMLP), row), t
    else:
        vn_spec, vn_rows = pl.BlockSpec((CHUNK, D_GMLP), full), CHUNK
    return pl.pallas_call(
        _gmlp_body,
        grid=(t // CHUNK,),
        in_specs=[pl.BlockSpec((CHUNK, D_GMLP), row), pl.BlockSpec((CHUNK, D_GMLP), row),
                  pl.BlockSpec((1, D_GMLP), full), pl.BlockSpec((1, D_GMLP), full),
                  pl.BlockSpec(ws.shape, lambda i: (0, 0, 0)), pl.BlockSpec(bs_t.shape, full)],
        out_specs=[pl.BlockSpec((CHUNK, D_GMLP), row), vn_spec],
        out_shape=[jax.ShapeDtypeStruct((t, D_GMLP), F32), jax.ShapeDtypeStruct((vn_rows, D_GMLP), F32)],
        compiler_params=_params("arbitrary"),
        name="gmlp",
    )(u, v, g, beta, ws, bs_t)


def _router_body(y_ref, g_ref, wr_ref, br_ref, xn_ref, comb_ref):
    hn = _rms(y_ref[...], g_ref[...], 1e-6)
    xn_ref[...] = hn.astype(BF16)
    logits = jnp.dot(hn, wr_ref[...], preferred_element_type=F32, precision=lax.Precision.HIGHEST) + br_ref[...]
    lane = lax.broadcasted_iota(jnp.int32, logits.shape, 1)
    lane_f = lane.astype(F32)
    far = float(LANES)
    is_g = (lane >= N_EXPERTS) & (lane < N_EXPERTS + E_GROUPS)
    gl = jnp.where(is_g, logits, -jnp.inf)
    gmax = jnp.max(gl, axis=1, keepdims=True)
    g_lane = jnp.min(jnp.where(gl == gmax, lane_f, far), axis=1, keepdims=True)
    g_w = 1.0 / jnp.sum(jnp.where(is_g, jnp.exp(gl - gmax), 0.0), axis=1, keepdims=True)
    g_sel = g_lane.astype(jnp.int32) - N_EXPERTS
    in_grp = (lane < N_EXPERTS) & ((lane // E_PER_GROUP) == g_sel)
    el = jnp.where(in_grp, logits, -jnp.inf)
    v1 = jnp.max(el, axis=1, keepdims=True)
    i1 = jnp.min(jnp.where(el == v1, lane_f, far), axis=1, keepdims=True)
    el2 = jnp.where(lane_f == i1, -jnp.inf, el)
    v2 = jnp.max(el2, axis=1, keepdims=True)
    i2 = jnp.min(jnp.where(el2 == v2, lane_f, far), axis=1, keepdims=True)
    r = jnp.exp(v2 - v1)
    w1 = 1.0 / (1.0 + r)
    w2 = r * w1
    comb_ref[...] = g_w * (jnp.where(lane_f == i1, w1, 0.0) + jnp.where(lane_f == i2, w2, 0.0))


def _router(y, g, wr, br):
    t = y.shape[0]
    tm = _row_tile(t, 512)
    row = lambda i: (i, 0)
    full = lambda i: (0, 0)
    return pl.pallas_call(
        _router_body,
        grid=(t // tm,),
        in_specs=[pl.BlockSpec((tm, D_MODEL), row), pl.BlockSpec((1, D_MODEL), full),
                  pl.BlockSpec(wr.shape, full), pl.BlockSpec((1, LANES), full)],
        out_specs=[pl.BlockSpec((tm, D_MODEL), row), pl.BlockSpec((tm, LANES), row)],
        out_shape=[jax.ShapeDtypeStruct((t, D_MODEL), BF16), jax.ShapeDtypeStruct((t, LANES), F32)],
        compiler_params=_params("parallel"),
        name="router",
    )(y, g, wr, br)


def _moe_dense_body(x_ref, c_ref, res_ref, wg_ref, wu_ref, wd_ref, o_ref):
    e = pl.program_id(1)

    @pl.when(e == 0)
    def _():
        o_ref[...] = res_ref[...]

    x = x_ref[...]
    hg = _dot(x, wg_ref[...])
    hid = (_silu(hg) * _dot(x, wu_ref[...])).astype(BF16)
    lane = lax.broadcasted_iota(jnp.int32, c_ref.shape, 1)
    ce = jnp.sum(jnp.where(lane == e, c_ref[...], 0.0), axis=1, keepdims=True)
    o_ref[...] += ce * _dot(hid, wd_ref[...])


def _moe_dense(xn, comb, res, wg, wu, wd):
    t = xn.shape[0]
    tm = _row_tile(t, 2048)
    row = lambda i, e: (i, 0)
    return pl.pallas_call(
        _moe_dense_body,
        grid=(t // tm, N_EXPERTS),
        in_specs=[pl.BlockSpec((tm, D_MODEL), row), pl.BlockSpec((tm, LANES), row),
                  pl.BlockSpec((tm, D_MODEL), row),
                  pl.BlockSpec((None, D_MODEL, D_EXPERT), lambda i, e: (e, 0, 0)),
                  pl.BlockSpec((None, D_MODEL, D_EXPERT), lambda i, e: (e, 0, 0)),
                  pl.BlockSpec((None, D_EXPERT, D_MODEL), lambda i, e: (e, 0, 0))],
        out_specs=pl.BlockSpec((tm, D_MODEL), row),
        out_shape=jax.ShapeDtypeStruct((t, D_MODEL), F32),
        compiler_params=_params("parallel", "arbitrary"),
        name="moe_dense",
    )(xn, comb, res, wg, wu, wd)


def _final_norm_body(x_ref, g_ref, o_ref):
    o_ref[...] = _rms(x_ref[...], g_ref[...], 1e-6)


def _final_norm(x, g):
    t = x.shape[0]
    tm = _row_tile(t, 1024)
    return pl.pallas_call(
        _final_norm_body,
        grid=(t // tm,),
        in_specs=[pl.BlockSpec((tm, D_MODEL), lambda i: (i, 0)), pl.BlockSpec((1, D_MODEL), lambda i: (0, 0))],
        out_specs=pl.BlockSpec((tm, D_MODEL), lambda i: (i, 0)),
        out_shape=jax.ShapeDtypeStruct((t, D_MODEL), F32),
        compiler_params=_params("parallel"),
        name="final_norm",
    )(x, g)


def _row(v):
    return v.reshape(1, -1).astype(F32)


def _pad_lanes(v, n=LANES):
    return jnp.pad(v, [(0, 0)] * (v.ndim - 1) + [(0, n - v.shape[-1])])


def kernel(x_prompt, x_sample, cache_attn_k, cache_attn_v, state_conf_conv, state_ssm_conv, state_ssm, page_table,
           norm_mix_g, norm_moe_g, norm_final_g,
           even_w_in, attn_lam_q1, attn_lam_k1, attn_lam_q2, attn_lam_k2, attn_subln_g,
           conf_conv_w, conf_conv_b, conf_ln_g, conf_ln_b, even_w_out,
           odd_w_in, ssm_conv_w, ssm_conv_b, ssm_dt_bias, ssm_a_log, ssm_d, ssm_norm_g,
           gmlp_ln_g, gmlp_ln_b, gmlp_ws, gmlp_bs, odd_w_out,
           moe_rg_w, moe_rg_b, moe_re_w, moe_re_b, moe_w_gate, moe_w_up, moe_w_down):
    bp, lp, _ = x_prompt.shape
    db, ls, _ = x_sample.shape
    assert bp == 1 and lp % CHUNK == 0 and (db * ls) % CHUNK == 0 and CHUNK % ls == 0
    depth = norm_mix_g.shape[0]
    page = cache_attn_k.shape[2]
    yp = x_prompt.reshape(bp * lp, D_MODEL)
    ys = x_sample.reshape(db * ls, D_MODEL)
    outs_p = {k: [] for k in ("k", "v", "conf", "sconv", "ssm", "gmlp")}
    outs_s = {k: [] for k in ("k", "v", "conf", "sconv", "ssm", "gmlp")}

    for layer in range(depth):
        g_mix = _row(norm_mix_g[layer])
        if layer % 2 == 0:
            e = layer // 2
            lambda_init = 0.8 - 0.6 * math.exp(-0.3 * layer)
            w_in = _split_weight(even_w_in[e].astype(F32))
            w_out = (_split_weight(even_w_out[e, :D_ATT].astype(F32))
                     + _split_weight(even_w_out[e, D_ATT:].astype(F32)))
            lamv = _pad_lanes(jnp.stack([attn_lam_q1[e], attn_lam_k1[e], attn_lam_q2[e], attn_lam_k2[e]]).astype(F32))
            lamv = jnp.pad(lamv, [(0, 4), (0, 0)])
            subln = _row(attn_subln_g[e])
            cw = jnp.pad(conf_conv_w[e].astype(F32), [(0, CONF_HALO - CONF_W), (0, 0)])
            cb, lg, lb = _row(conf_conv_b[e]), _row(conf_ln_g[e]), _row(conf_ln_b[e])

            qx, kx, k, v, vx, glu = _even_in(yp, g_mix, *w_in)
            attn = _attn_prompt(lamv, qx, kx, vx, subln, lambda_init)
            conv = _conf_conv(glu, glu, cw, cb, lg, lb, _row_tile(lp, 256), True)
            yp = _out_proj(attn, conv, yp, *w_out)
            outs_p["k"].append(k.reshape(bp, lp, H_A, HEAD_W))
            outs_p["v"].append(v.reshape(bp, lp, H_A, HEAD_W))
            outs_p["conf"].append(glu[lp - (CONF_W - 1):].reshape(bp, CONF_W - 1, D_CONF))

            qx, kx, k, v, _, glu = _even_in(ys, g_mix, *w_in)
            qparts = qx.astype(F32).reshape(2, db, ls, H_A, 4, DK_A)
            zero = jnp.zeros((db, ls, H_A, DK_A), F32)

            def q_rows(part):
                both = jnp.stack([jnp.concatenate([qparts[0, ..., part, :], zero], axis=-1),
                                  jnp.concatenate([zero, qparts[1, ..., part, :]], axis=-1)], axis=1)
                return both.transpose(0, 1, 3, 2, 4).reshape(db * 2 * H_A * ls, HEAD_W)

            n_pool = cache_attn_k.shape[1]
            attn = _attn_sample(page_table + e * n_pool, lamv, q_rows(0), q_rows(2),
                                k.reshape(db * ls * H_A, HEAD_W), v.reshape(db * ls * H_A, HEAD_W), subln,
                                cache_attn_k.reshape(-1, page * H_A, HEAD_W),
                                cache_attn_v.reshape(-1, page * H_A, HEAD_W), ls, lambda_init)
            hist = state_conf_conv[e].astype(F32)
            halo = jnp.pad(hist, [(0, 0), (CONF_HALO - (CONF_W - 1), 0), (0, 0)])
            conv = _conf_conv(glu, halo, cw, cb, lg, lb, ls, False)
            ys = _out_proj(attn, conv, ys, *w_out)
            outs_s["k"].append(k.reshape(db, ls, H_A, HEAD_W))
            outs_s["v"].append(v.reshape(db, ls, H_A, HEAD_W))
            outs_s["conf"].append(jnp.concatenate([hist, glu.reshape(db, ls, D_CONF)], axis=1)[:, ls:])
        else:
            o = layer // 2
            w = odd_w_in[o]
            i1, i2, i3, i4 = D_INNER, D_INNER + D_XBC, D_INNER + D_XBC + H_C, D_INNER + D_XBC + H_C + D_GMLP
            w_dt = w[:, i2:i3]
            w_in = jnp.concatenate([w[:, :i1], w[:, i1:i2], w[:, i3:i4], w[:, i4:],
                                    jnp.repeat(w_dt, P_C, axis=1), _pad_lanes(w_dt)], axis=1).astype(BF16)
            wy = odd_w_out[o, :D_INNER].astype(BF16)
            wd_ = odd_w_out[o, D_INNER:].astype(BF16)
            cw = jnp.pad(ssm_conv_w[o].astype(F32), [(0, SSM_HALO - SSM_CONV_W), (0, 0)])
            cb = _row(ssm_conv_b[o])
            dtb, alog = _pad_lanes(_row(ssm_dt_bias[o])), _pad_lanes(_row(ssm_a_log[o]))
            dtbe, aloge = _row(jnp.repeat(ssm_dt_bias[o], P_C)), _row(jnp.repeat(ssm_a_log[o], P_C))
            de = _row(jnp.repeat(ssm_d[o], P_C))
            ng = _row(ssm_norm_g[o])
            lg, lb = _row(gmlp_ln_g[o]), _row(gmlp_ln_b[o])
            ws = gmlp_ws[o].astype(F32)
            bs_t = _pad_lanes(gmlp_bs[o].astype(F32).T)

            z, xbc, u, v, dte, dt = _odd_in(yp, g_mix, w_in)
            prev0 = jnp.zeros((bp, SSM_HALO, D_XBC), F32)
            h00 = jnp.zeros((bp, D_INNER, N_C), F32)
            y, h_new = _ssd(z, xbc, dte, dt, prev0, h00, cw, cb, dtbe, aloge, dtb, alog, de, ng, bp, CHUNK)
            d_out, vn = _gmlp(u, v, lg, lb, ws, bs_t, False)
            yp = _out_proj(y, d_out, yp, wy, wd_)
            outs_p["sconv"].append(xbc[lp - (SSM_CONV_W - 1):].reshape(bp, SSM_CONV_W - 1, D_XBC))
            outs_p["ssm"].append(h_new.reshape(bp, H_C, P_C, N_C))
            outs_p["gmlp"].append(vn.reshape(bp, CHUNK, D_GMLP))

            z, xbc, u, v, dte, dt = _odd_in(ys, g_mix, w_in)
            hist = state_ssm_conv[o].astype(F32)
            prev = jnp.pad(hist, [(0, 0), (SSM_HALO - (SSM_CONV_W - 1), 0), (0, 0)])
            y, h_new = _ssd(z, xbc, dte, dt, prev, state_ssm[o].astype(F32).reshape(db, D_INNER, N_C),
                            cw, cb, dtbe, aloge, dtb, alog, de, ng, db, ls)
            eye = jnp.eye(CHUNK // ls, dtype=F32)
            ws_s = jnp.stack([jnp.kron(eye, ws[g, :ls, :ls]) for g in range(G_D)])
            bs_s = jnp.tile(bs_t[:ls], (CHUNK // ls, 1))
            d_out, vn = _gmlp(u, v, lg, lb, ws_s, bs_s, True)
            ys = _out_proj(y, d_out, ys, wy, wd_)
            outs_s["sconv"].append(jnp.concatenate([hist, xbc.reshape(db, ls, D_XBC)], axis=1)[:, ls:])
            outs_s["ssm"].append(h_new.reshape(db, H_C, P_C, N_C))
            outs_s["gmlp"].append(vn.reshape(db, ls, D_GMLP))

        g_moe = _row(norm_moe_g[layer])
        wr = _pad_lanes(jnp.concatenate([moe_re_w[layer], moe_rg_w[layer]], axis=1).astype(F32))
        br = _pad_lanes(_row(jnp.concatenate([moe_re_b[layer], moe_rg_b[layer]])))
        wg = moe_w_gate[layer].astype(BF16)
        wu = moe_w_up[layer].astype(BF16)
        wd = moe_w_down[layer].astype(BF16)
        xn, comb = _router(yp, g_moe, wr, br)
        yp = _moe_dense(xn, comb, yp, wg, wu, wd)
        xn, comb = _router(ys, g_moe, wr, br)
        ys = _moe_dense(xn, comb, ys, wg, wu, wd)

    g_fin = _row(norm_final_g)
    y_prompt = _final_norm(yp, g_fin).reshape(bp, lp, D_MODEL)
    y_sample = _final_norm(ys, g_fin).reshape(db, ls, D_MODEL)
    st = lambda xs: jnp.stack(xs)
    return (y_prompt, y_sample,
            st(outs_p["k"]), st(outs_p["v"]), st(outs_p["conf"]), st(outs_p["sconv"]), st(outs_p["ssm"]),
            st(outs_p["gmlp"]),
            st(outs_s["k"]), st(outs_s["v"]), st(outs_s["conf"]), st(outs_s["sconv"]), st(outs_s["ssm"]),
            st(outs_s["gmlp"]))
```

```python
import functools
import math

import jax
import jax.numpy as jnp
from jax import lax
from jax.experimental import pallas as pl
from jax.experimental.pallas import tpu as pltpu

F32 = jnp.float32
BF16 = jnp.bfloat16

D_MODEL = 1024
H_A = 4
DK_A = 64
HEAD_W = 128
D_ATT = H_A * HEAD_W
D_CONF = 512
CONF_W = 31
CONF_HALO = 32
H_C = 16
P_C = 64
D_INNER = H_C * P_C
G_C = 2
N_C = 128
D_XBC = D_INNER + 2 * G_C * N_C
SSM_CONV_W = 4
SSM_HALO = 8
CHUNK = 128
G_D = 4
D_GMLP = 512
E_GROUPS = 4
E_PER_GROUP = 8
N_EXPERTS = 32
D_EXPERT = 256
NEG_INF = -1e30
LOG2_E = math.log2(math.e)
LANES = 128
VMEM_LIMIT = 56 * 1024 * 1024


def _params(*sem):
    return pltpu.CompilerParams(dimension_semantics=sem, vmem_limit_bytes=VMEM_LIMIT)


def _sigmoid(x):
    return 1.0 / (1.0 + jnp.exp(-x))


def _silu(x):
    return x * _sigmoid(x)


def _gelu_tanh(x):
    return 0.5 * x * (1.0 + jnp.tanh(math.sqrt(2.0 / math.pi) * (x + 0.044715 * (x * x * x))))


def _softplus(x):
    return jnp.maximum(x, 0.0) + jnp.log(1.0 + jnp.exp(-jnp.abs(x)))


def _rms(x, g, eps):
    return x * lax.rsqrt(jnp.mean(x * x, axis=-1, keepdims=True) + eps) * g


def _dot(a, b):
    return jnp.dot(a, b, preferred_element_type=F32)


def _dot_nt(a, b):
    return lax.dot_general(a, b, (((1,), (1,)), ((), ())), preferred_element_type=F32)


def _row_tile(t, cap):
    tm = min(t, cap)
    while t % tm:
        tm //= 2
    return tm


def _hi_f32(x):
    return x.astype(BF16).astype(F32)


def _split2(x):
    hi = x.astype(BF16)
    return hi, (x - hi.astype(F32)).astype(BF16)


def _dot3(a_hi, a_lo, w_hi, w_lo):
    return _dot(a_hi, w_hi) + _dot(a_hi, w_lo) + _dot(a_lo, w_hi)


def _split_weight_body(w_ref, hi_ref, lo_ref):
    hi, lo = _split2(w_ref[...])
    hi_ref[...] = hi
    lo_ref[...] = lo


def _split_weight(w):
    r, c = w.shape
    tr = _row_tile(r, 256)
    spec = pl.BlockSpec((tr, c), lambda i: (i, 0))
    return pl.pallas_call(
        _split_weight_body,
        grid=(r // tr,),
        in_specs=[spec],
        out_specs=[spec, spec],
        out_shape=[jax.ShapeDtypeStruct((r, c), BF16), jax.ShapeDtypeStruct((r, c), BF16)],
        compiler_params=_params("parallel"),
        name="split_weight",
    )(w)


def _even_in_body(x_ref, g_ref, wh_ref, wl_ref, qx_ref, kx_ref, k_ref, v_ref, vx_ref, glu_ref):
    h_hi, h_lo = _split2(_rms(x_ref[...], g_ref[...], 1e-6))

    def proj(c0, n):
        return _dot3(h_hi, h_lo, wh_ref[:, c0:c0 + n], wl_ref[:, c0:c0 + n])

    q = proj(0, D_ATT) * (DK_A ** -0.5 * LOG2_E)
    k = proj(D_ATT, D_ATT)
    k_ref[...] = k
    first = lax.broadcasted_iota(jnp.int32, (q.shape[0], HEAD_W), 1) < DK_A
    for hd in range(H_A):
        for src, dst, is_q in ((q, qx_ref, True), (k, kx_ref, False)):
            blk = src[:, hd * HEAD_W:(hd + 1) * HEAD_W]
            rot = pltpu.roll(blk, DK_A, 1)
            bh, rh = _hi_f32(blk), _hi_f32(rot)
            bl, rl = blk - bh, rot - rh
            lo_a = hd * 2 * HEAD_W
            lo_b = lo_a + HEAD_W
            if is_q:
                parts = ((jnp.where(first, bh, rh), jnp.where(first, bl, 0.0)),
                         (jnp.where(first, rh, bh), jnp.where(first, rl, 0.0)))
            else:
                parts = ((jnp.where(first, bh, rl), jnp.where(first, bh, 0.0)),
                         (jnp.where(first, rh, bl), jnp.where(first, rh, 0.0)))
            for i in range(2):
                dst[i, :, lo_a:lo_b] = parts[i][0].astype(BF16)
                dst[i, :, lo_b:lo_b + HEAD_W] = parts[i][1].astype(BF16)
    v = proj(2 * D_ATT, D_ATT)
    v_ref[...] = v
    v_hi, v_lo = _split2(v)
    for hd in range(H_A):
        vx_ref[:, 2 * hd * HEAD_W:(2 * hd + 1) * HEAD_W] = v_hi[:, hd * HEAD_W:(hd + 1) * HEAD_W]
        vx_ref[:, (2 * hd + 1) * HEAD_W:(2 * hd + 2) * HEAD_W] = v_lo[:, hd * HEAD_W:(hd + 1) * HEAD_W]
    a = proj(3 * D_ATT, D_CONF)
    gate = proj(3 * D_ATT + D_CONF, D_CONF)
    glu_ref[...] = a * _sigmoid(gate)


def _even_in(x, g, wh, wl):
    t = x.shape[0]
    tm = _row_tile(t, 256)
    row = lambda i: (i, 0)
    full = lambda i: (0, 0)
    wide = lambda i: (0, i, 0)
    return pl.pallas_call(
        _even_in_body,
        grid=(t // tm,),
        in_specs=[pl.BlockSpec((tm, D_MODEL), row), pl.BlockSpec((1, D_MODEL), full),
                  pl.BlockSpec(wh.shape, full), pl.BlockSpec(wl.shape, full)],
        out_specs=[pl.BlockSpec((2, tm, 2 * D_ATT), wide), pl.BlockSpec((2, tm, 2 * D_ATT), wide),
                   pl.BlockSpec((tm, D_ATT), row), pl.BlockSpec((tm, D_ATT), row),
                   pl.BlockSpec((tm, 2 * D_ATT), row), pl.BlockSpec((tm, D_CONF), row)],
        out_shape=[jax.ShapeDtypeStruct((2, t, 2 * D_ATT), BF16), jax.ShapeDtypeStruct((2, t, 2 * D_ATT), BF16),
                   jax.ShapeDtypeStruct((t, D_ATT), F32), jax.ShapeDtypeStruct((t, D_ATT), F32),
                   jax.ShapeDtypeStruct((t, 2 * D_ATT), BF16), jax.ShapeDtypeStruct((t, D_CONF), F32)],
        compiler_params=_params("parallel"),
        name="even_in",
    )(x, g, wh, wl)


def _lambda(lamv_ref, lambda_init):
    t1 = jnp.sum(lamv_ref[0:1, :] * lamv_ref[1:2, :], axis=1, keepdims=True)
    t2 = jnp.sum(lamv_ref[2:3, :] * lamv_ref[3:4, :], axis=1, keepdims=True)
    return jnp.exp(t1) - jnp.exp(t2) + lambda_init


def _online_update(s, v_tiles, m_scr, l_scr, acc_scr):
    rows = s.shape[0]
    m_prev = m_scr[...]
    m_new = jnp.maximum(m_prev, jnp.max(s, axis=1, keepdims=True))
    alpha = jnp.exp2(m_prev - m_new)
    p = jnp.exp2(s - m_new)
    l_scr[...] = alpha * l_scr[...] + jnp.sum(p, axis=1, keepdims=True)
    p_hi, p_lo = _split2(p)
    p_hl = jnp.concatenate([p_hi, p_lo], axis=0)
    pv = None
    for off, width, vt in v_tiles:
        v_hi, v_lo = _split2(vt)
        both = _dot(p_hl[:, off:off + width], v_hi)
        d = both[:rows] + both[rows:] + _dot(p_hi[:, off:off + width], v_lo)
        pv = d if pv is None else pv + d
    acc_scr[...] = alpha * acc_scr[...] + pv
    m_scr[...] = m_new


def _attn_prompt_body(lamv_ref, q_ref, k_ref, v_ref, g_ref, o_ref, m_scr, l_scr, acc_scr, *, tq, rc, lambda_init):
    i = pl.program_id(1)
    m_scr[...] = jnp.full(m_scr.shape, NEG_INF, F32)
    l_scr[...] = jnp.zeros(l_scr.shape, F32)
    acc_scr[...] = jnp.zeros(acc_scr.shape, F32)

    def tile(j, masked):
        start = pl.multiple_of(j * tq, tq)
        vx = v_ref[pl.ds(start, tq), :]
        for mp in range(2):
            kt = k_ref[mp, pl.ds(start, tq), :]
            for c in range(tq // rc):
                r0 = mp * tq + c * rc
                s = _dot_nt(q_ref[mp, c * rc:(c + 1) * rc, :], kt)
                if masked:
                    row = lax.broadcasted_iota(jnp.int32, s.shape, 0) + c * rc
                    col = lax.broadcasted_iota(jnp.int32, s.shape, 1)
                    s = jnp.where(col <= row, s, NEG_INF)
                tiles = [s[:, b * LANES:(b + 1) * LANES] for b in range(tq // LANES)]
                top = functools.reduce(jnp.maximum, tiles)
                m_prev = m_scr[r0:r0 + rc, :]
                m_new = jnp.maximum(m_prev, jnp.max(top, axis=1, keepdims=True))
                alpha = jnp.exp2(m_prev - m_new)
                ps = [jnp.exp2(t - m_new) for t in tiles]
                l_scr[r0:r0 + rc, :] = (alpha * l_scr[r0:r0 + rc, :]
                                        + jnp.sum(functools.reduce(jnp.add, ps), axis=1, keepdims=True))
                p_hi, p_lo = _split2(jnp.concatenate(ps, axis=1))
                pv = _dot(p_hi, vx)
                acc_scr[r0:r0 + rc, :] = (alpha * acc_scr[r0:r0 + rc, :] + pv[:, :HEAD_W] + pv[:, HEAD_W:]
                                          + _dot(p_lo, vx[:, :HEAD_W]))
                m_scr[r0:r0 + rc, :] = m_new

    def body(j, carry):
        tile(j, False)
        return carry

    lax.fori_loop(0, i, body, 0)
    tile(i, True)
    lam = _lambda(lamv_ref, lambda_init)
    o1 = acc_scr[0:tq, :] / l_scr[0:tq, :]
    o2 = acc_scr[tq:, :] / l_scr[tq:, :]
    o_ref[...] = _rms(o1 - lam * o2, g_ref[...], 1e-5) * (1.0 - lambda_init)


def _attn_prompt(lamv, qx, kx, vx, subln_g, lambda_init):
    t = vx.shape[0]
    tq = _row_tile(t, 1024)
    rc = tq
    body = functools.partial(_attn_prompt_body, tq=tq, rc=rc, lambda_init=lambda_init)
    once = pl.Buffered(1)
    return pl.pallas_call(
        body,
        grid=(H_A, t // tq),
        in_specs=[pl.BlockSpec((8, LANES), lambda h, i: (0, 0)),
                  pl.BlockSpec((2, tq, 2 * HEAD_W), lambda h, i: (0, i, h)),
                  pl.BlockSpec((2, t, 2 * HEAD_W), lambda h, i: (0, 0, h), pipeline_mode=once),
                  pl.BlockSpec((t, 2 * HEAD_W), lambda h, i: (0, h), pipeline_mode=once),
                  pl.BlockSpec((1, HEAD_W), lambda h, i: (0, 0))],
        out_specs=pl.BlockSpec((tq, HEAD_W), lambda h, i: (i, h)),
        out_shape=jax.ShapeDtypeStruct((t, D_ATT), F32),
        scratch_shapes=[pltpu.VMEM((2 * tq, LANES), F32), pltpu.VMEM((2 * tq, LANES), F32),
                        pltpu.VMEM((2 * tq, HEAD_W), F32)],
        compiler_params=_params("parallel", "arbitrary"),
        name="attn_prompt",
    )(lamv, qx, kx, vx, subln_g)


def _attn_sample_body(pt_ref, lamv_ref, qh_ref, ql_ref, kn_ref, vn_ref, g_ref, *rest, n_pg, lq, lambda_init):
    k_refs = rest[:n_pg]
    v_refs = rest[n_pg:2 * n_pg]
    o_ref, m_scr, l_scr, acc_scr = rest[2 * n_pg:]
    j = pl.program_id(1)
    rows = 2 * H_A * lq
    pr = k_refs[0].shape[0]

    @pl.when(j == 0)
    def _():
        m_scr[...] = jnp.full(m_scr.shape, NEG_INF, F32)
        l_scr[...] = jnp.zeros(l_scr.shape, F32)
        acc_scr[...] = jnp.zeros(acc_scr.shape, F32)

    qhl = jnp.concatenate([qh_ref[...], ql_ref[...]], axis=0).astype(BF16)
    qh = qhl[:rows]
    row_i = lax.broadcasted_iota(jnp.int32, (rows, pr), 0)
    col_i = lax.broadcasted_iota(jnp.int32, (rows, pr), 1)
    same_head = ((row_i // lq) % H_A) == (col_i % H_A)

    def scores(k):
        kh, kl = _split2(k)
        both = _dot_nt(qhl, kh)
        return both[:rows] + both[rows:] + _dot_nt(qh, kl)

    s = jnp.concatenate([jnp.where(same_head, scores(k_refs[p][...]), NEG_INF) for p in range(n_pg)], axis=1)
    _online_update(s, [(p * pr, pr, v_refs[p][...]) for p in range(n_pg)], m_scr, l_scr, acc_scr)

    @pl.when(j == pl.num_programs(1) - 1)
    def _():
        pad = jnp.zeros((pr - lq * H_A, HEAD_W), F32)
        sn = scores(jnp.concatenate([kn_ref[...], pad], axis=0))
        causal = (col_i // H_A) <= (row_i % lq)
        sn = jnp.where(same_head & causal, sn, NEG_INF)
        _online_update(sn, [(0, pr, jnp.concatenate([vn_ref[...], pad], axis=0))], m_scr, l_scr, acc_scr)
        lam = _lambda(lamv_ref, lambda_init)
        o = acc_scr[...] / l_scr[...]
        d = o[:H_A * lq] - lam * o[H_A * lq:]
        y = _rms(d, g_ref[...], 1e-5) * (1.0 - lambda_init)
        o_ref[...] = jnp.concatenate([y[h * lq:(h + 1) * lq] for h in range(H_A)], axis=1)


def _attn_sample(page_table, lamv, q_hi, q_lo, k_new, v_new, subln_g, cache_k, cache_v, lq, lambda_init):
    nb, n_pages = page_table.shape
    pr = cache_k.shape[1]
    n_pg = math.gcd(n_pages, 8)
    rows = 2 * H_A * lq
    body = functools.partial(_attn_sample_body, n_pg=n_pg, lq=lq, lambda_init=lambda_init)

    def page_spec(p):
        return pl.BlockSpec((None, pr, HEAD_W), lambda b, j, pt, p=p: (pt[b, j * n_pg + p], 0, 0))

    per_seq = lambda b, j, pt: (b, 0)
    grid_spec = pltpu.PrefetchScalarGridSpec(
        num_scalar_prefetch=1,
        grid=(nb, n_pages // n_pg),
        in_specs=[pl.BlockSpec((8, LANES), lambda b, j, pt: (0, 0)),
                  pl.BlockSpec((rows, HEAD_W), per_seq), pl.BlockSpec((rows, HEAD_W), per_seq),
                  pl.BlockSpec((lq * H_A, HEAD_W), per_seq), pl.BlockSpec((lq * H_A, HEAD_W), per_seq),
                  pl.BlockSpec((1, HEAD_W), lambda b, j, pt: (0, 0))]
                 + [page_spec(p) for p in range(n_pg)] + [page_spec(p) for p in range(n_pg)],
        out_specs=pl.BlockSpec((lq, D_ATT), per_seq),
        scratch_shapes=[pltpu.VMEM((rows, 1), F32), pltpu.VMEM((rows, 1), F32),
                        pltpu.VMEM((rows, HEAD_W), F32)],
    )
    return pl.pallas_call(
        body,
        grid_spec=grid_spec,
        out_shape=jax.ShapeDtypeStruct((nb * lq, D_ATT), F32),
        compiler_params=_params("parallel", "arbitrary"),
        name="attn_sample",
    )(page_table, lamv, q_hi, q_lo, k_new, v_new, subln_g, *([cache_k] * n_pg), *([cache_v] * n_pg))


def _conf_conv_body(x_ref, halo_ref, w_ref, b_ref, g_ref, beta_ref, o_ref, xp_scr, *, tt, rb, halo_is_x):
    halo = halo_ref[...]
    if halo_is_x:
        halo = jnp.where(pl.program_id(0) == 0, 0.0, halo)
    xp_scr[0:CONF_HALO, :] = halo
    xp_scr[CONF_HALO:CONF_HALO + tt, :] = x_ref[...]
    first = CONF_HALO - (CONF_W - 1)
    for r0 in range(0, tt, rb):
        acc = jnp.broadcast_to(b_ref[...], (rb, D_CONF))
        for j in range(CONF_W):
            acc = acc + w_ref[j:j + 1, :] * xp_scr[r0 + first + j:r0 + first + j + rb, :]
        mu = jnp.mean(acc, axis=-1, keepdims=True)
        cen = acc - mu
        var = jnp.mean(cen * cen, axis=-1, keepdims=True)
        y = cen * lax.rsqrt(var + 1e-5) * g_ref[...] + beta_ref[...]
        o_ref[r0:r0 + rb, :] = _silu(y)


def _conf_conv(x, halo_src, w, b, g, beta, tt, halo_is_x):
    t = x.shape[0]
    rb = min(tt, 64)
    body = functools.partial(_conf_conv_body, tt=tt, rb=rb, halo_is_x=halo_is_x)
    if halo_is_x:
        per = tt // CONF_HALO
        halo_spec = pl.BlockSpec((CONF_HALO, D_CONF), lambda i: (jnp.maximum(i * per - 1, 0), 0))
    else:
        halo_spec = pl.BlockSpec((None, CONF_HALO, D_CONF), lambda i: (i, 0, 0))
    full = lambda i: (0, 0)
    return pl.pallas_call(
        body,
        grid=(t // tt,),
        in_specs=[pl.BlockSpec((tt, D_CONF), lambda i: (i, 0)), halo_spec,
                  pl.BlockSpec(w.shape, full), pl.BlockSpec((1, D_CONF), full),
                  pl.BlockSpec((1, D_CONF), full), pl.BlockSpec((1, D_CONF), full)],
        out_specs=pl.BlockSpec((tt, D_CONF), lambda i: (i, 0)),
        out_shape=jax.ShapeDtypeStruct((t, D_CONF), F32),
        scratch_shapes=[pltpu.VMEM((CONF_HALO + tt, D_CONF), F32)],
        compiler_params=_params("parallel"),
        name="conf_conv",
    )(x, halo_src, w, b, g, beta)


def _out_proj_body(a_ref, c_ref, res_ref, wa_ref, wc_ref, o_ref):
    o_ref[...] = (res_ref[...] + _dot(a_ref[...].astype(BF16), wa_ref[...])
                  + _dot(c_ref[...].astype(BF16), wc_ref[...]))


def _out_proj_split_body(a_ref, c_ref, res_ref, wah_ref, wal_ref, wch_ref, wcl_ref, o_ref):
    a_hi, a_lo = _split2(a_ref[...])
    c_hi, c_lo = _split2(c_ref[...])
    o_ref[...] = (res_ref[...] + _dot3(a_hi, a_lo, wah_ref[...], wal_ref[...])
                  + _dot3(c_hi, c_lo, wch_ref[...], wcl_ref[...]))


def _out_proj(a, c, res, *weights):
    t = a.shape[0]
    tm = _row_tile(t, 512)
    row = lambda i: (i, 0)
    full = lambda i: (0, 0)
    body = _out_proj_body if len(weights) == 2 else _out_proj_split_body
    return pl.pallas_call(
        body,
        grid=(t // tm,),
        in_specs=[pl.BlockSpec((tm, a.shape[1]), row), pl.BlockSpec((tm, c.shape[1]), row),
                  pl.BlockSpec((tm, D_MODEL), row)] + [pl.BlockSpec(w.shape, full) for w in weights],
        out_specs=pl.BlockSpec((tm, D_MODEL), row),
        out_shape=jax.ShapeDtypeStruct((t, D_MODEL), F32),
        compiler_params=_params("parallel"),
        name="out_proj",
    )(a, c, res, *weights)


_O_Z = 0
_O_XBC = _O_Z + D_INNER
_O_U = _O_XBC + D_XBC
_O_V = _O_U + D_GMLP
_O_DTE = _O_V + D_GMLP
_O_DT = _O_DTE + D_INNER
_O_END = _O_DT + LANES


def _odd_in_body(x_ref, g_ref, w_ref, z_ref, xbc_ref, u_ref, v_ref, dte_ref, dt_ref):
    h = _rms(x_ref[...], g_ref[...], 1e-6).astype(BF16)
    z_ref[...] = _dot(h, w_ref[:, _O_Z:_O_XBC])
    xbc_ref[...] = _dot(h, w_ref[:, _O_XBC:_O_U])
    u_ref[...] = _dot(h, w_ref[:, _O_U:_O_V])
    v_ref[...] = _dot(h, w_ref[:, _O_V:_O_DTE])
    dte_ref[...] = _dot(h, w_ref[:, _O_DTE:_O_DT])
    dt_ref[...] = _dot(h, w_ref[:, _O_DT:_O_END])


def _odd_in(x, g, w):
    t = x.shape[0]
    tm = _row_tile(t, 512)
    row = lambda i: (i, 0)
    full = lambda i: (0, 0)
    widths = (D_INNER, D_XBC, D_GMLP, D_GMLP, D_INNER, LANES)
    return pl.pallas_call(
        _odd_in_body,
        grid=(t // tm,),
        in_specs=[pl.BlockSpec((tm, D_MODEL), row), pl.BlockSpec((1, D_MODEL), full),
                  pl.BlockSpec(w.shape, full)],
        out_specs=[pl.BlockSpec((tm, n), row) for n in widths],
        out_shape=[jax.ShapeDtypeStruct((t, n), F32) for n in widths],
        compiler_params=_params("parallel"),
        name="odd_in",
    )(x, g, w)


def _split3(x):
    hi = x.astype(BF16)
    r1 = x - hi.astype(F32)
    mid = r1.astype(BF16)
    lo = (r1 - mid.astype(F32)).astype(BF16)
    return hi, mid, lo


def _cumsum_rows(tril_b, x):
    hi, mid, lo = _split3(x)
    return _dot(tril_b, hi) + _dot(tril_b, mid) + _dot(tril_b, lo)


def _ssd_body(z_ref, xbc_ref, dte_ref, dt_ref, prev_ref, h0_ref, cw_ref, cb_ref, dtbe_ref, aloge_ref,
              dtb_ref, alog_ref, de_ref, ng_ref, y_ref, hl_ref, xp_scr, ht_scr, *, rows_in):
    c = pl.program_id(1)
    pairs = D_INNER // LANES

    @pl.when(c == 0)
    def _():
        xp_scr[0:SSM_HALO, :] = prev_ref[...]
        for k in range(pairs):
            ht_scr[:, k * LANES:(k + 1) * LANES] = h0_ref[k * LANES:(k + 1) * LANES, :].T

    def padded(ref):
        v = ref[...]
        if rows_in == CHUNK:
            return v
        return jnp.concatenate([v, jnp.zeros((CHUNK - rows_in, v.shape[1]), F32)], axis=0)

    xp_scr[SSM_HALO:SSM_HALO + CHUNK, :] = padded(xbc_ref)
    first = SSM_HALO - (SSM_CONV_W - 1)
    conv = jnp.broadcast_to(cb_ref[...], (CHUNK, D_XBC))
    for j in range(SSM_CONV_W):
        conv = conv + cw_ref[j:j + 1, :] * xp_scr[first + j:first + j + CHUNK, :]
    xc = _silu(conv)
    xs = xc[:, :D_INNER]
    dt_e = _softplus(padded(dte_ref) + dtbe_ref[...])
    dt_h = _softplus(padded(dt_ref) + dtb_ref[...])
    if rows_in != CHUNK:
        live = lax.broadcasted_iota(jnp.int32, (CHUNK, 1), 0) < rows_in
        xs = jnp.where(live, xs, 0.0)
        dt_e = jnp.where(live, dt_e, 0.0)
        dt_h = jnp.where(live, dt_h, 0.0)

    r_i = lax.broadcasted_iota(jnp.int32, (CHUNK, CHUNK), 0)
    c_i = lax.broadcasted_iota(jnp.int32, (CHUNK, CHUNK), 1)
    causal = r_i >= c_i
    tril_b = jnp.where(causal, 1.0, 0.0).astype(BF16)
    cs_e = _cumsum_rows(tril_b, dt_e * (-jnp.exp(aloge_ref[...])))
    cs_h = _cumsum_rows(tril_b, dt_h * (-jnp.exp(alog_ref[...])))
    cs_t = cs_h.T

    xdt = xs * dt_e
    decay_in = jnp.exp(cs_e)
    xdt_b = xdt.astype(BF16)
    xde_b = (xdt * jnp.exp(cs_e[CHUNK - 1:CHUNK, :] - cs_e)).astype(BF16)
    ht = ht_scr[...]
    ht_b = ht.astype(BF16)
    lane_lo = lax.broadcasted_iota(jnp.int32, (CHUNK, LANES), 1) < P_C
    ys = []
    for g in range(G_C):
        bm = xc[:, D_INNER + g * N_C:D_INNER + (g + 1) * N_C]
        cm = xc[:, D_INNER + (G_C + g) * N_C:D_INNER + (G_C + g + 1) * N_C]
        bm_b = bm.astype(BF16)
        cm_b = cm.astype(BF16)
        cb = _dot_nt(cm_b, bm_b)
        bm_t = bm.T.astype(BF16)
        for k in range(pairs // G_C):
            kk = g * (pairs // G_C) + k
            cols = slice(kk * LANES, (kk + 1) * LANES)
            mats = []
            for hh in (2 * kk, 2 * kk + 1):
                seg = cs_h[:, hh:hh + 1] - cs_t[hh:hh + 1, :]
                mats.append((cb * jnp.exp(jnp.where(causal, seg, NEG_INF))).astype(BF16))
            xp2 = xdt_b[:, cols]
            y_diag = jnp.where(lane_lo, _dot(mats[0], xp2), _dot(mats[1], xp2))
            y_off = _dot(cm_b, ht_b[:, cols]) * decay_in[:, cols]
            ht_scr[:, cols] = ht[:, cols] * decay_in[CHUNK - 1:CHUNK, cols] + _dot(bm_t, xde_b[:, cols])
            ys.append(y_diag + y_off + de_ref[:, cols] * xs[:, cols])
    z = padded(z_ref)
    half = D_INNER // G_C
    outs = []
    for g in range(G_C):
        yg = jnp.concatenate(ys[g * (pairs // G_C):(g + 1) * (pairs // G_C)], axis=1)
        yg = yg * _silu(z[:, g * half:(g + 1) * half])
        outs.append(_rms(yg, ng_ref[:, g * half:(g + 1) * half], 1e-5))
    y_ref[...] = jnp.concatenate(outs, axis=1)[:rows_in]
    xp_scr[0:SSM_HALO, :] = xp_scr[CHUNK:CHUNK + SSM_HALO, :]

    @pl.when(c == pl.num_programs(1) - 1)
    def _():
        for k in range(pairs):
            hl_ref[k * LANES:(k + 1) * LANES, :] = ht_scr[:, k * LANES:(k + 1) * LANES].T


def _ssd(z, xbc, dte, dt, prev, h0, cw, cb, dtbe, aloge, dtb, alog, de, ng, nb, rows_in):
    t = z.shape[0]
    nc = t // nb // rows_in
    body = functools.partial(_ssd_body, rows_in=rows_in)
    row = lambda b, c: (b * nc + c, 0)
    full = lambda b, c: (0, 0)
    seq = lambda b, c: (b, 0, 0)
    vec = lambda a: pl.BlockSpec(a.shape, full)
    return pl.pallas_call(
        body,
        grid=(nb, nc),
        in_specs=[pl.BlockSpec((rows_in, D_INNER), row), pl.BlockSpec((rows_in, D_XBC), row),
                  pl.BlockSpec((rows_in, D_INNER), row), pl.BlockSpec((rows_in, LANES), row),
                  pl.BlockSpec((None, SSM_HALO, D_XBC), seq), pl.BlockSpec((None, D_INNER, N_C), seq),
                  vec(cw), vec(cb), vec(dtbe), vec(aloge), vec(dtb), vec(alog), vec(de), vec(ng)],
        out_specs=[pl.BlockSpec((rows_in, D_INNER), row), pl.BlockSpec((None, D_INNER, N_C), seq)],
        out_shape=[jax.ShapeDtypeStruct((t, D_INNER), F32), jax.ShapeDtypeStruct((nb, D_INNER, N_C), F32)],
        scratch_shapes=[pltpu.VMEM((SSM_HALO + CHUNK, D_XBC), F32), pltpu.VMEM((N_C, D_INNER), F32)],
        compiler_params=_params("parallel", "arbitrary"),
        name="ssd",
    )(z, xbc, dte, dt, prev, h0, cw, cb, dtbe, aloge, dtb, alog, de, ng)


def _gmlp_body(u_ref, v_ref, g_ref, beta_ref, ws_ref, bs_ref, d_ref, vn_ref):
    gu = _gelu_tanh(u_ref[...])
    gv = _gelu_tanh(v_ref[...])
    mu = jnp.mean(gv, axis=-1, keepdims=True)
    cen = gv - mu
    var = jnp.mean(cen * cen, axis=-1, keepdims=True)
    vn = cen * lax.rsqrt(var + 1e-5) * g_ref[...] + beta_ref[...]
    vn_ref[...] = vn
    r_i = lax.broadcasted_iota(jnp.int32, (CHUNK, CHUNK), 0)
    c_i = lax.broadcasted_iota(jnp.int32, (CHUNK, CHUNK), 1)
    dg = D_GMLP // G_D
    for g in range(G_D):
        wm = jnp.where(r_i >= c_i, ws_ref[g], 0.0).astype(BF16)
        s = _dot(wm, vn[:, g * dg:(g + 1) * dg].astype(BF16)) + bs_ref[:, g:g + 1]
        d_ref[:, g * dg:(g + 1) * dg] = gu[:, g * dg:(g + 1) * dg] * s


def _gmlp(u, v, g, beta, ws, bs_t, keep_all_vn):
    t = u.shape[0]
    row = lambda i: (i, 0)
    full = lambda i: (0, 0)
    if keep_all_vn:
        vn_spec, vn_rows = pl.BlockSpec((CHUNK, D_GMLP), row), t
    else:
        vn_spec, vn_rows = pl.BlockSpec((CHUNK, D_GMLP), full), CHUNK
    return pl.pallas_call(
        _gmlp_body,
        grid=(t // CHUNK,),
        in_specs=[pl.BlockSpec((CHUNK, D_GMLP), row), pl.BlockSpec((CHUNK, D_GMLP), row),
                  pl.BlockSpec((1, D_GMLP), full), pl.BlockSpec((1, D_GMLP), full),
                  pl.BlockSpec(ws.shape, lambda i: (0, 0, 0)), pl.BlockSpec(bs_t.shape, full)],
        out_specs=[pl.BlockSpec((CHUNK, D_GMLP), row), vn_spec],
        out_shape=[jax.ShapeDtypeStruct((t, D_GMLP), F32), jax.ShapeDtypeStruct((vn_rows, D_GMLP), F32)],
        compiler_params=_params("arbitrary"),
        name="gmlp",
    )(u, v, g, beta, ws, bs_t)


def _router_body(y_ref, g_ref, wr_ref, br_ref, xn_ref, comb_ref):
    hn = _rms(y_ref[...], g_ref[...], 1e-6)
    xn_ref[...] = hn.astype(BF16)
    logits = jnp.dot(hn, wr_ref[...], preferred_element_type=F32, precision=lax.Precision.HIGHEST) + br_ref[...]
    lane = lax.broadcasted_iota(jnp.int32, logits.shape, 1)
    lane_f = lane.astype(F32)
    far = float(LANES)
    is_g = (lane >= N_EXPERTS) & (lane < N_EXPERTS + E_GROUPS)
    gl = jnp.where(is_g, logits, -jnp.inf)
    gmax = jnp.max(gl, axis=1, keepdims=True)
    g_lane = jnp.min(jnp.where(gl == gmax, lane_f, far), axis=1, keepdims=True)
    g_w = 1.0 / jnp.sum(jnp.where(is_g, jnp.exp(gl - gmax), 0.0), axis=1, keepdims=True)
    g_sel = g_lane.astype(jnp.int32) - N_EXPERTS
    in_grp = (lane < N_EXPERTS) & ((lane // E_PER_GROUP) == g_sel)
    el = jnp.where(in_grp, logits, -jnp.inf)
    v1 = jnp.max(el, axis=1, keepdims=True)
    i1 = jnp.min(jnp.where(el == v1, lane_f, far), axis=1, keepdims=True)
    el2 = jnp.where(lane_f == i1, -jnp.inf, el)
    v2 = jnp.max(el2, axis=1, keepdims=True)
    i2 = jnp.min(jnp.where(el2 == v2, lane_f, far), axis=1, keepdims=True)
    r = jnp.exp(v2 - v1)
    w1 = 1.0 / (1.0 + r)
    w2 = r * w1
    comb_ref[...] = g_w * (jnp.where(lane_f == i1, w1, 0.0) + jnp.where(lane_f == i2, w2, 0.0))


def _router(y, g, wr, br):
    t = y.shape[0]
    tm = _row_tile(t, 512)
    row = lambda i: (i, 0)
    full = lambda i: (0, 0)
    return pl.pallas_call(
        _router_body,
        grid=(t // tm,),
        in_specs=[pl.BlockSpec((tm, D_MODEL), row), pl.BlockSpec((1, D_MODEL), full),
                  pl.BlockSpec(wr.shape, full), pl.BlockSpec((1, LANES), full)],
        out_specs=[pl.BlockSpec((tm, D_MODEL), row), pl.BlockSpec((tm, LANES), row)],
        out_shape=[jax.ShapeDtypeStruct((t, D_MODEL), BF16), jax.ShapeDtypeStruct((t, LANES), F32)],
        compiler_params=_params("parallel"),
        name="router",
    )(y, g, wr, br)


def _moe_dense_body(x_ref, c_ref, res_ref, wg_ref, wu_ref, wd_ref, o_ref):
    e = pl.program_id(1)

    @pl.when(e == 0)
    def _():
        o_ref[...] = res_ref[...]

    x = x_ref[...]
    hg = _dot(x, wg_ref[...])
    hid = (_silu(hg) * _dot(x, wu_ref[...])).astype(BF16)
    lane = lax.broadcasted_iota(jnp.int32, c_ref.shape, 1)
    ce = jnp.sum(jnp.where(lane == e, c_ref[...], 0.0), axis=1, keepdims=True)
    o_ref[...] += ce * _dot(hid, wd_ref[...])


def _moe_dense(xn, comb, res, wg, wu, wd):
    t = xn.shape[0]
    tm = _row_tile(t, 2048)
    row = lambda i, e: (i, 0)
    return pl.pallas_call(
        _moe_dense_body,
        grid=(t // tm, N_EXPERTS),
        in_specs=[pl.BlockSpec((tm, D_MODEL), row), pl.BlockSpec((tm, LANES), row),
                  pl.BlockSpec((tm, D_MODEL), row),
                  pl.BlockSpec((None, D_MODEL, D_EXPERT), lambda i, e: (e, 0, 0)),
                  pl.BlockSpec((None, D_MODEL, D_EXPERT), lambda i, e: (e, 0, 0)),
                  pl.BlockSpec((None, D_EXPERT, D_MODEL), lambda i, e: (e, 0, 0))],
        out_specs=pl.BlockSpec((tm, D_MODEL), row),
        out_shape=jax.ShapeDtypeStruct((t, D_MODEL), F32),
        compiler_params=_params("parallel", "arbitrary"),
        name="moe_dense",
    )(xn, comb, res, wg, wu, wd)


def _final_norm_body(x_ref, g_ref, o_ref):
    o_ref[...] = _rms(x_ref[...], g_ref[...], 1e-6)


def _final_norm(x, g):
    t = x.shape[0]
    tm = _row_tile(t, 1024)
    return pl.pallas_call(
        _final_norm_body,
        grid=(t // tm,),
        in_specs=[pl.BlockSpec((tm, D_MODEL), lambda i: (i, 0)), pl.BlockSpec((1, D_MODEL), lambda i: (0, 0))],
        out_specs=pl.BlockSpec((tm, D_MODEL), lambda i: (i, 0)),
        out_shape=jax.ShapeDtypeStruct((t, D_MODEL), F32),
        compiler_params=_params("parallel"),
        name="final_norm",
    )(x, g)


def _row(v):
    return v.reshape(1, -1).astype(F32)


def _pad_lanes(v, n=LANES):
    return jnp.pad(v, [(0, 0)] * (v.ndim - 1) + [(0, n - v.shape[-1])])


def kernel(x_prompt, x_sample, cache_attn_k, cache_attn_v, state_conf_conv, state_ssm_conv, state_ssm, page_table,
           norm_mix_g, norm_moe_g, norm_final_g,
           even_w_in, attn_lam_q1, attn_lam_k1, attn_lam_q2, attn_lam_k2, attn_subln_g,
           conf_conv_w, conf_conv_b, conf_ln_g, conf_ln_b, even_w_out,
           odd_w_in, ssm_conv_w, ssm_conv_b, ssm_dt_bias, ssm_a_log, ssm_d, ssm_norm_g,
           gmlp_ln_g, gmlp_ln_b, gmlp_ws, gmlp_bs, odd_w_out,
           moe_rg_w, moe_rg_b, moe_re_w, moe_re_b, moe_w_gate, moe_w_up, moe_w_down):
    bp, lp, _ = x_prompt.shape
    db, ls, _ = x_sample.shape
    assert bp == 1 and lp % CHUNK == 0 and (db * ls) % CHUNK == 0 and CHUNK % ls == 0
    depth = norm_mix_g.shape[0]
    page = cache_attn_k.shape[2]
    yp = x_prompt.reshape(bp * lp, D_MODEL)
    ys = x_sample.reshape(db * ls, D_MODEL)
    outs_p = {k: [] for k in ("k", "v", "conf", "sconv", "ssm", "gmlp")}
    outs_s = {k: [] for k in ("k", "v", "conf", "sconv", "ssm", "gmlp")}

    for layer in range(depth):
        g_mix = _row(norm_mix_g[layer])
        if layer % 2 == 0:
            e = layer // 2
            lambda_init = 0.8 - 0.6 * math.exp(-0.3 * layer)
            w_in = _split_weight(even_w_in[e].astype(F32))
            w_out = (_split_weight(even_w_out[e, :D_ATT].astype(F32))
                     + _split_weight(even_w_out[e, D_ATT:].astype(F32)))
            lamv = _pad_lanes(jnp.stack([attn_lam_q1[e], attn_lam_k1[e], attn_lam_q2[e], attn_lam_k2[e]]).astype(F32))
            lamv = jnp.pad(lamv, [(0, 4), (0, 0)])
            subln = _row(attn_subln_g[e])
            cw = jnp.pad(conf_conv_w[e].astype(F32), [(0, CONF_HALO - CONF_W), (0, 0)])
            cb, lg, lb = _row(conf_conv_b[e]), _row(conf_ln_g[e]), _row(conf_ln_b[e])

            qx, kx, k, v, vx, glu = _even_in(yp, g_mix, *w_in)
            attn = _attn_prompt(lamv, qx, kx, vx, subln, lambda_init)
            conv = _conf_conv(glu, glu, cw, cb, lg, lb, _row_tile(lp, 256), True)
            yp = _out_proj(attn, conv, yp, *w_out)
            outs_p["k"].append(k.reshape(bp, lp, H_A, HEAD_W))
            outs_p["v"].append(v.reshape(bp, lp, H_A, HEAD_W))
            outs_p["conf"].append(glu[lp - (CONF_W - 1):].reshape(bp, CONF_W - 1, D_CONF))

            qx, kx, k, v, _, glu = _even_in(ys, g_mix, *w_in)
            qparts = qx.astype(F32).reshape(2, db, ls, H_A, 4, DK_A)
            zero = jnp.zeros((db, ls, H_A, DK_A), F32)

            def q_rows(part):
                both = jnp.stack([jnp.concatenate([qparts[0, ..., part, :], zero], axis=-1),
                                  jnp.concatenate([zero, qparts[1, ..., part, :]], axis=-1)], axis=1)
                return both.transpose(0, 1, 3, 2, 4).reshape(db * 2 * H_A * ls, HEAD_W)

            n_pool = cache_attn_k.shape[1]
            attn = _attn_sample(page_table + e * n_pool, lamv, q_rows(0), q_rows(2),
                                k.reshape(db * ls * H_A, HEAD_W), v.reshape(db * ls * H_A, HEAD_W), subln,
                                cache_attn_k.reshape(-1, page * H_A, HEAD_W),
                                cache_attn_v.reshape(-1, page * H_A, HEAD_W), ls, lambda_init)
            hist = state_conf_conv[e].astype(F32)
            halo = jnp.pad(hist, [(0, 0), (CONF_HALO - (CONF_W - 1), 0), (0, 0)])
            conv = _conf_conv(glu, halo, cw, cb, lg, lb, ls, False)
            ys = _out_proj(attn, conv, ys, *w_out)
            outs_s["k"].append(k.reshape(db, ls, H_A, HEAD_W))
            outs_s["v"].append(v.reshape(db, ls, H_A, HEAD_W))
            outs_s["conf"].append(jnp.concatenate([hist, glu.reshape(db, ls, D_CONF)], axis=1)[:, ls:])
        else:
            o = layer // 2
            w = odd_w_in[o]
            i1, i2, i3, i4 = D_INNER, D_INNER + D_XBC, D_INNER + D_XBC + H_C, D_INNER + D_XBC + H_C + D_GMLP
            w_dt = w[:, i2:i3]
            w_in = jnp.concatenate([w[:, :i1], w[:, i1:i2], w[:, i3:i4], w[:, i4:],
                                    jnp.repeat(w_dt, P_C, axis=1), _pad_lanes(w_dt)], axis=1).astype(BF16)
            wy = odd_w_out[o, :D_INNER].astype(BF16)
            wd_ = odd_w_out[o, D_INNER:].astype(BF16)
            cw = jnp.pad(ssm_conv_w[o].astype(F32), [(0, SSM_HALO - SSM_CONV_W), (0, 0)])
            cb = _row(ssm_conv_b[o])
            dtb, alog = _pad_lanes(_row(ssm_dt_bias[o])), _pad_lanes(_row(ssm_a_log[o]))
            dtbe, aloge = _row(jnp.repeat(ssm_dt_bias[o], P_C)), _row(jnp.repeat(ssm_a_log[o], P_C))
            de = _row(jnp.repeat(ssm_d[o], P_C))
            ng = _row(ssm_norm_g[o])
            lg, lb = _row(gmlp_ln_g[o]), _row(gmlp_ln_b[o])
            ws = gmlp_ws[o].astype(F32)
            bs_t = _pad_lanes(gmlp_bs[o].astype(F32).T)

            z, xbc, u, v, dte, dt = _odd_in(yp, g_mix, w_in)
            prev0 = jnp.zeros((bp, SSM_HALO, D_XBC), F32)
            h00 = jnp.zeros((bp, D_INNER, N_C), F32)
            y, h_new = _ssd(z, xbc, dte, dt, prev0, h00, cw, cb, dtbe, aloge, dtb, alog, de, ng, bp, CHUNK)
            d_out, vn = _gmlp(u, v, lg, lb, ws, bs_t, False)
            yp = _out_proj(y, d_out, yp, wy, wd_)
            outs_p["sconv"].append(xbc[lp - (SSM_CONV_W - 1):].reshape(bp, SSM_CONV_W - 1, D_XBC))
            outs_p["ssm"].append(h_new.reshape(bp, H_C, P_C, N_C))
            outs_p["gmlp"].append(vn.reshape(bp, CHUNK, D_GMLP))

            z, xbc, u, v, dte, dt = _odd_in(ys, g_mix, w_in)
            hist = state_ssm_conv[o].astype(F32)
            prev = jnp.pad(hist, [(0, 0), (SSM_HALO - (SSM_CONV_W - 1), 0), (0, 0)])
            y, h_new = _ssd(z, xbc, dte, dt, prev, state_ssm[o].astype(F32).reshape(db, D_INNER, N_C),
                            cw, cb, dtbe, aloge, dtb, alog, de, ng, db, ls)
            eye = jnp.eye(CHUNK // ls, dtype=F32)
            ws_s = jnp.stack([jnp.kron(eye, ws[g, :ls, :ls]) for g in range(G_D)])
            bs_s = jnp.tile(bs_t[:ls], (CHUNK // ls, 1))
            d_out, vn = _gmlp(u, v, lg, lb, ws_s, bs_s, True)
            ys = _out_proj(y, d_out, ys, wy, wd_)
            outs_s["sconv"].append(jnp.concatenate([hist, xbc.reshape(db, ls, D_XBC)], axis=1)[:, ls:])
            outs_s["ssm"].append(h_new.reshape(db, H_C, P_C, N_C))
            outs_s["gmlp"].append(vn.reshape(db, ls, D_GMLP))

        g_moe = _row(norm_moe_g[layer])
        wr = _pad_lanes(jnp.concatenate([moe_re_w[layer], moe_rg_w[layer]], axis=1).astype(F32))
        br = _pad_lanes(_row(jnp.concatenate([moe_re_b[layer], moe_rg_b[layer]])))
        wg = moe_w_gate[layer].astype(BF16)
        wu = moe_w_up[layer].astype(BF16)
        wd = moe_w_down[layer].astype(BF16)
        xn, comb = _router(yp, g_moe, wr, br)
        yp = _moe_dense(xn, comb, yp, wg, wu, wd)
        xn, comb = _router(ys, g_moe, wr, br)
        ys = _moe_dense(xn, comb, ys, wg, wu, wd)

    g_fin = _row(norm_final_g)
    y_prompt = _final_norm(yp, g_fin).reshape(bp, lp, D_MODEL)
    y_sample = _final_norm(ys, g_fin).reshape(db, ls, D_MODEL)
    st = lambda xs: jnp.stack(xs)
    return (y_prompt, y_sample,
            st(outs_p["k"]), st(outs_p["v"]), st(outs_p["conf"]), st(outs_p["sconv"]), st(outs_p["ssm"]),
            st(outs_p["gmlp"]),
            st(outs_s["k"]), st(outs_s["v"]), st(outs_s["conf"]), st(outs_s["sconv"]), st(outs_s["ssm"]),
            st(outs_s["gmlp"]))
```

```python
import functools
import math

import jax
import jax.numpy as jnp
from jax import lax
from jax.experimental import pallas as pl
from jax.experimental.pallas import tpu as pltpu

F32 = jnp.float32
BF16 = jnp.bfloat16

D_MODEL = 1024
H_A = 4
DK_A = 64
HEAD_W = 128
D_ATT = H_A * HEAD_W
D_CONF = 512
CONF_W = 31
CONF_HALO = 32
H_C = 16
P_C = 64
D_INNER = H_C * P_C
G_C = 2
N_C = 128
D_XBC = D_INNER + 2 * G_C * N_C
SSM_CONV_W = 4
SSM_HALO = 8
CHUNK = 128
G_D = 4
D_GMLP = 512
E_GROUPS = 4
E_PER_GROUP = 8
N_EXPERTS = 32
D_EXPERT = 256
MOE_EXPERTS_PER_STEP = 4
NEG_INF = -1e30
LOG2_E = math.log2(math.e)
LANES = 128
VMEM_LIMIT = 56 * 1024 * 1024


def _params(*sem):
    return pltpu.CompilerParams(dimension_semantics=sem, vmem_limit_bytes=VMEM_LIMIT)


def _sigmoid(x):
    return 1.0 / (1.0 + jnp.exp(-x))


def _silu(x):
    return x * _sigmoid(x)


def _gelu_tanh(x):
    return 0.5 * x * (1.0 + jnp.tanh(math.sqrt(2.0 / math.pi) * (x + 0.044715 * (x * x * x))))


def _softplus(x):
    return jnp.maximum(x, 0.0) + jnp.log(1.0 + jnp.exp(-jnp.abs(x)))


def _rms(x, g, eps):
    return x * lax.rsqrt(jnp.mean(x * x, axis=-1, keepdims=True) + eps) * g


def _dot(a, b):
    return jnp.dot(a, b, preferred_element_type=F32)


def _dot_nt(a, b):
    return lax.dot_general(a, b, (((1,), (1,)), ((), ())), preferred_element_type=F32)


def _row_tile(t, cap):
    tm = min(t, cap)
    while t % tm:
        tm //= 2
    return tm


def _hi_f32(x):
    return x.astype(BF16).astype(F32)


def _split2(x):
    hi = x.astype(BF16)
    return hi, (x - hi.astype(F32)).astype(BF16)


def _dot3(a_hi, a_lo, w_hi, w_lo):
    return _dot(a_hi, w_hi) + _dot(a_hi, w_lo) + _dot(a_lo, w_hi)


def _split_weight_body(w_ref, hi_ref, lo_ref):
    hi, lo = _split2(w_ref[...])
    hi_ref[...] = hi
    lo_ref[...] = lo


def _split_weight(w):
    r, c = w.shape
    tr = _row_tile(r, 256)
    spec = pl.BlockSpec((tr, c), lambda i: (i, 0))
    return pl.pallas_call(
        _split_weight_body,
        grid=(r // tr,),
        in_specs=[spec],
        out_specs=[spec, spec],
        out_shape=[jax.ShapeDtypeStruct((r, c), BF16), jax.ShapeDtypeStruct((r, c), BF16)],
        compiler_params=_params("parallel"),
        name="split_weight",
    )(w)


def _even_in_body(x_ref, g_ref, wh_ref, wl_ref, qx_ref, kx_ref, k_ref, v_ref, vx_ref, glu_ref):
    h_hi, h_lo = _split2(_rms(x_ref[...], g_ref[...], 1e-6))

    def proj(c0, n):
        return _dot3(h_hi, h_lo, wh_ref[:, c0:c0 + n], wl_ref[:, c0:c0 + n])

    q = proj(0, D_ATT) * (DK_A ** -0.5 * LOG2_E)
    k = proj(D_ATT, D_ATT)
    k_ref[...] = k
    first = lax.broadcasted_iota(jnp.int32, (q.shape[0], HEAD_W), 1) < DK_A
    for hd in range(H_A):
        for src, dst, is_q in ((q, qx_ref, True), (k, kx_ref, False)):
            blk = src[:, hd * HEAD_W:(hd + 1) * HEAD_W]
            rot = pltpu.roll(blk, DK_A, 1)
            bh, rh = _hi_f32(blk), _hi_f32(rot)
            bl, rl = blk - bh, rot - rh
            lo_a = hd * 2 * HEAD_W
            lo_b = lo_a + HEAD_W
            if is_q:
                parts = ((jnp.where(first, bh, rh), jnp.where(first, bl, 0.0)),
                         (jnp.where(first, rh, bh), jnp.where(first, rl, 0.0)))
            else:
                parts = ((jnp.where(first, bh, rl), jnp.where(first, bh, 0.0)),
                         (jnp.where(first, rh, bl), jnp.where(first, rh, 0.0)))
            for i in range(2):
                dst[i, :, lo_a:lo_b] = parts[i][0].astype(BF16)
                dst[i, :, lo_b:lo_b + HEAD_W] = parts[i][1].astype(BF16)
    v = proj(2 * D_ATT, D_ATT)
    v_ref[...] = v
    v_hi, v_lo = _split2(v)
    for hd in range(H_A):
        vx_ref[:, 2 * hd * HEAD_W:(2 * hd + 1) * HEAD_W] = v_hi[:, hd * HEAD_W:(hd + 1) * HEAD_W]
        vx_ref[:, (2 * hd + 1) * HEAD_W:(2 * hd + 2) * HEAD_W] = v_lo[:, hd * HEAD_W:(hd + 1) * HEAD_W]
    a = proj(3 * D_ATT, D_CONF)
    gate = proj(3 * D_ATT + D_CONF, D_CONF)
    glu_ref[...] = a * _sigmoid(gate)


def _even_in(x, g, wh, wl):
    t = x.shape[0]
    tm = _row_tile(t, 256)
    row = lambda i: (i, 0)
    full = lambda i: (0, 0)
    wide = lambda i: (0, i, 0)
    return pl.pallas_call(
        _even_in_body,
        grid=(t // tm,),
        in_specs=[pl.BlockSpec((tm, D_MODEL), row), pl.BlockSpec((1, D_MODEL), full),
                  pl.BlockSpec(wh.shape, full), pl.BlockSpec(wl.shape, full)],
        out_specs=[pl.BlockSpec((2, tm, 2 * D_ATT), wide), pl.BlockSpec((2, tm, 2 * D_ATT), wide),
                   pl.BlockSpec((tm, D_ATT), row), pl.BlockSpec((tm, D_ATT), row),
                   pl.BlockSpec((tm, 2 * D_ATT), row), pl.BlockSpec((tm, D_CONF), row)],
        out_shape=[jax.ShapeDtypeStruct((2, t, 2 * D_ATT), BF16), jax.ShapeDtypeStruct((2, t, 2 * D_ATT), BF16),
                   jax.ShapeDtypeStruct((t, D_ATT), F32), jax.ShapeDtypeStruct((t, D_ATT), F32),
                   jax.ShapeDtypeStruct((t, 2 * D_ATT), BF16), jax.ShapeDtypeStruct((t, D_CONF), F32)],
        compiler_params=_params("parallel"),
        name="even_in",
    )(x, g, wh, wl)


def _lambda(lamv_ref, lambda_init):
    t1 = jnp.sum(lamv_ref[0:1, :] * lamv_ref[1:2, :], axis=1, keepdims=True)
    t2 = jnp.sum(lamv_ref[2:3, :] * lamv_ref[3:4, :], axis=1, keepdims=True)
    return jnp.exp(t1) - jnp.exp(t2) + lambda_init


def _online_update(s, v_tiles, m_scr, l_scr, acc_scr):
    rows = s.shape[0]
    m_prev = m_scr[...]
    m_new = jnp.maximum(m_prev, jnp.max(s, axis=1, keepdims=True))
    alpha = jnp.exp2(m_prev - m_new)
    p = jnp.exp2(s - m_new)
    l_scr[...] = alpha * l_scr[...] + jnp.sum(p, axis=1, keepdims=True)
    p_hi, p_lo = _split2(p)
    p_hl = jnp.concatenate([p_hi, p_lo], axis=0)
    pv = None
    for off, width, vt in v_tiles:
        v_hi, v_lo = _split2(vt)
        both = _dot(p_hl[:, off:off + width], v_hi)
        d = both[:rows] + both[rows:] + _dot(p_hi[:, off:off + width], v_lo)
        pv = d if pv is None else pv + d
    acc_scr[...] = alpha * acc_scr[...] + pv
    m_scr[...] = m_new


def _attn_prompt_body(lamv_ref, q_ref, k_ref, v_ref, g_ref, o_ref, m_scr, l_scr, acc_scr, *, tq, rc, lambda_init):
    i = pl.program_id(1)
    m_scr[...] = jnp.full(m_scr.shape, NEG_INF, F32)
    l_scr[...] = jnp.zeros(l_scr.shape, F32)
    acc_scr[...] = jnp.zeros(acc_scr.shape, F32)

    def tile(j, masked):
        start = pl.multiple_of(j * tq, tq)
        vx = v_ref[pl.ds(start, tq), :]
        for mp in range(2):
            kt = k_ref[mp, pl.ds(start, tq), :]
            for c in range(tq // rc):
                r0 = mp * tq + c * rc
                s = _dot_nt(q_ref[mp, c * rc:(c + 1) * rc, :], kt)
                if masked:
                    row = lax.broadcasted_iota(jnp.int32, s.shape, 0) + c * rc
                    col = lax.broadcasted_iota(jnp.int32, s.shape, 1)
                    s = jnp.where(col <= row, s, NEG_INF)
                tiles = [s[:, b * LANES:(b + 1) * LANES] for b in range(tq // LANES)]
                top = functools.reduce(jnp.maximum, tiles)
                m_prev = m_scr[r0:r0 + rc, :]
                m_new = jnp.maximum(m_prev, jnp.max(top, axis=1, keepdims=True))
                alpha = jnp.exp2(m_prev - m_new)
                ps = [jnp.exp2(t - m_new) for t in tiles]
                l_scr[r0:r0 + rc, :] = (alpha * l_scr[r0:r0 + rc, :]
                                        + jnp.sum(functools.reduce(jnp.add, ps), axis=1, keepdims=True))
                p_hi, p_lo = _split2(jnp.concatenate(ps, axis=1))
                pv = _dot(p_hi, vx)
                acc_scr[r0:r0 + rc, :] = (alpha * acc_scr[r0:r0 + rc, :] + pv[:, :HEAD_W] + pv[:, HEAD_W:]
                                          + _dot(p_lo, vx[:, :HEAD_W]))
                m_scr[r0:r0 + rc, :] = m_new

    def body(j, carry):
        tile(j, False)
        return carry

    lax.fori_loop(0, i, body, 0)
    tile(i, True)
    lam = _lambda(lamv_ref, lambda_init)
    o1 = acc_scr[0:tq, :] / l_scr[0:tq, :]
    o2 = acc_scr[tq:, :] / l_scr[tq:, :]
    o_ref[...] = _rms(o1 - lam * o2, g_ref[...], 1e-5) * (1.0 - lambda_init)


def _attn_prompt(lamv, qx, kx, vx, subln_g, lambda_init):
    t = vx.shape[0]
    tq = _row_tile(t, 1024)
    rc = tq
    body = functools.partial(_attn_prompt_body, tq=tq, rc=rc, lambda_init=lambda_init)
    once = pl.Buffered(1)
    return pl.pallas_call(
        body,
        grid=(H_A, t // tq),
        in_specs=[pl.BlockSpec((8, LANES), lambda h, i: (0, 0)),
                  pl.BlockSpec((2, tq, 2 * HEAD_W), lambda h, i: (0, i, h)),
                  pl.BlockSpec((2, t, 2 * HEAD_W), lambda h, i: (0, 0, h), pipeline_mode=once),
                  pl.BlockSpec((t, 2 * HEAD_W), lambda h, i: (0, h), pipeline_mode=once),
                  pl.BlockSpec((1, HEAD_W), lambda h, i: (0, 0))],
        out_specs=pl.BlockSpec((tq, HEAD_W), lambda h, i: (i, h)),
        out_shape=jax.ShapeDtypeStruct((t, D_ATT), F32),
        scratch_shapes=[pltpu.VMEM((2 * tq, LANES), F32), pltpu.VMEM((2 * tq, LANES), F32),
                        pltpu.VMEM((2 * tq, HEAD_W), F32)],
        compiler_params=_params("parallel", "arbitrary"),
        name="attn_prompt",
    )(lamv, qx, kx, vx, subln_g)


def _attn_sample_body(pt_ref, lamv_ref, qh_ref, ql_ref, kn_ref, vn_ref, g_ref, *rest, n_pg, lq, lambda_init):
    k_refs = rest[:n_pg]
    v_refs = rest[n_pg:2 * n_pg]
    o_ref, m_scr, l_scr, acc_scr = rest[2 * n_pg:]
    j = pl.program_id(1)
    rows = 2 * H_A * lq
    pr = k_refs[0].shape[0]

    @pl.when(j == 0)
    def _():
        m_scr[...] = jnp.full(m_scr.shape, NEG_INF, F32)
        l_scr[...] = jnp.zeros(l_scr.shape, F32)
        acc_scr[...] = jnp.zeros(acc_scr.shape, F32)

    qhl = jnp.concatenate([qh_ref[...], ql_ref[...]], axis=0).astype(BF16)
    qh = qhl[:rows]
    row_i = lax.broadcasted_iota(jnp.int32, (rows, pr), 0)
    col_i = lax.broadcasted_iota(jnp.int32, (rows, pr), 1)
    same_head = ((row_i // lq) % H_A) == (col_i % H_A)

    def scores(k):
        kh, kl = _split2(k)
        both = _dot_nt(qhl, kh)
        return both[:rows] + both[rows:] + _dot_nt(qh, kl)

    s = jnp.concatenate([jnp.where(same_head, scores(k_refs[p][...]), NEG_INF) for p in range(n_pg)], axis=1)
    _online_update(s, [(p * pr, pr, v_refs[p][...]) for p in range(n_pg)], m_scr, l_scr, acc_scr)

    @pl.when(j == pl.num_programs(1) - 1)
    def _():
        pad = jnp.zeros((pr - lq * H_A, HEAD_W), F32)
        sn = scores(jnp.concatenate([kn_ref[...], pad], axis=0))
        causal = (col_i // H_A) <= (row_i % lq)
        sn = jnp.where(same_head & causal, sn, NEG_INF)
        _online_update(sn, [(0, pr, jnp.concatenate([vn_ref[...], pad], axis=0))], m_scr, l_scr, acc_scr)
        lam = _lambda(lamv_ref, lambda_init)
        o = acc_scr[...] / l_scr[...]
        d = o[:H_A * lq] - lam * o[H_A * lq:]
        y = _rms(d, g_ref[...], 1e-5) * (1.0 - lambda_init)
        o_ref[...] = jnp.concatenate([y[h * lq:(h + 1) * lq] for h in range(H_A)], axis=1)


def _attn_sample(page_table, lamv, q_hi, q_lo, k_new, v_new, subln_g, cache_k, cache_v, lq, lambda_init):
    nb, n_pages = page_table.shape
    pr = cache_k.shape[1]
    n_pg = math.gcd(n_pages, 8)
    rows = 2 * H_A * lq
    body = functools.partial(_attn_sample_body, n_pg=n_pg, lq=lq, lambda_init=lambda_init)

    def page_spec(p):
        return pl.BlockSpec((None, pr, HEAD_W), lambda b, j, pt, p=p: (pt[b, j * n_pg + p], 0, 0))

    per_seq = lambda b, j, pt: (b, 0)
    grid_spec = pltpu.PrefetchScalarGridSpec(
        num_scalar_prefetch=1,
        grid=(nb, n_pages // n_pg),
        in_specs=[pl.BlockSpec((8, LANES), lambda b, j, pt: (0, 0)),
                  pl.BlockSpec((rows, HEAD_W), per_seq), pl.BlockSpec((rows, HEAD_W), per_seq),
                  pl.BlockSpec((lq * H_A, HEAD_W), per_seq), pl.BlockSpec((lq * H_A, HEAD_W), per_seq),
                  pl.BlockSpec((1, HEAD_W), lambda b, j, pt: (0, 0))]
                 + [page_spec(p) for p in range(n_pg)] + [page_spec(p) for p in range(n_pg)],
        out_specs=pl.BlockSpec((lq, D_ATT), per_seq),
        scratch_shapes=[pltpu.VMEM((rows, 1), F32), pltpu.VMEM((rows, 1), F32),
                        pltpu.VMEM((rows, HEAD_W), F32)],
    )
    return pl.pallas_call(
        body,
        grid_spec=grid_spec,
        out_shape=jax.ShapeDtypeStruct((nb * lq, D_ATT), F32),
        compiler_params=_params("parallel", "arbitrary"),
        name="attn_sample",
    )(page_table, lamv, q_hi, q_lo, k_new, v_new, subln_g, *([cache_k] * n_pg), *([cache_v] * n_pg))


def _conf_conv_body(x_ref, halo_ref, w_ref, b_ref, g_ref, beta_ref, o_ref, xp_scr, *, tt, rb, halo_is_x):
    halo = halo_ref[...]
    if halo_is_x:
        halo = jnp.where(pl.program_id(0) == 0, 0.0, halo)
    xp_scr[0:CONF_HALO, :] = halo
    xp_scr[CONF_HALO:CONF_HALO + tt, :] = x_ref[...]
    first = CONF_HALO - (CONF_W - 1)
    for r0 in range(0, tt, rb):
        acc = jnp.broadcast_to(b_ref[...], (rb, D_CONF))
        for j in range(CONF_W):
            acc = acc + w_ref[j:j + 1, :] * xp_scr[r0 + first + j:r0 + first + j + rb, :]
        mu = jnp.mean(acc, axis=-1, keepdims=True)
        cen = acc - mu
        var = jnp.mean(cen * cen, axis=-1, keepdims=True)
        y = cen * lax.rsqrt(var + 1e-5) * g_ref[...] + beta_ref[...]
        o_ref[r0:r0 + rb, :] = _silu(y)


def _conf_conv(x, halo_src, w, b, g, beta, tt, halo_is_x):
    t = x.shape[0]
    rb = min(tt, 64)
    body = functools.partial(_conf_conv_body, tt=tt, rb=rb, halo_is_x=halo_is_x)
    if halo_is_x:
        per = tt // CONF_HALO
        halo_spec = pl.BlockSpec((CONF_HALO, D_CONF), lambda i: (jnp.maximum(i * per - 1, 0), 0))
    else:
        halo_spec = pl.BlockSpec((None, CONF_HALO, D_CONF), lambda i: (i, 0, 0))
    full = lambda i: (0, 0)
    return pl.pallas_call(
        body,
        grid=(t // tt,),
        in_specs=[pl.BlockSpec((tt, D_CONF), lambda i: (i, 0)), halo_spec,
                  pl.BlockSpec(w.shape, full), pl.BlockSpec((1, D_CONF), full),
                  pl.BlockSpec((1, D_CONF), full), pl.BlockSpec((1, D_CONF), full)],
        out_specs=pl.BlockSpec((tt, D_CONF), lambda i: (i, 0)),
        out_shape=jax.ShapeDtypeStruct((t, D_CONF), F32),
        scratch_shapes=[pltpu.VMEM((CONF_HALO + tt, D_CONF), F32)],
        compiler_params=_params("parallel"),
        name="conf_conv",
    )(x, halo_src, w, b, g, beta)


def _out_proj_body(a_ref, c_ref, res_ref, wa_ref, wc_ref, o_ref):
    o_ref[...] = (res_ref[...] + _dot(a_ref[...].astype(BF16), wa_ref[...])
                  + _dot(c_ref[...].astype(BF16), wc_ref[...]))


def _out_proj_split_body(a_ref, c_ref, res_ref, wah_ref, wal_ref, wch_ref, wcl_ref, o_ref):
    a_hi, a_lo = _split2(a_ref[...])
    c_hi, c_lo = _split2(c_ref[...])
    o_ref[...] = (res_ref[...] + _dot3(a_hi, a_lo, wah_ref[...], wal_ref[...])
                  + _dot3(c_hi, c_lo, wch_ref[...], wcl_ref[...]))


def _out_proj(a, c, res, *weights):
    t = a.shape[0]
    tm = _row_tile(t, 512)
    row = lambda i: (i, 0)
    full = lambda i: (0, 0)
    body = _out_proj_body if len(weights) == 2 else _out_proj_split_body
    return pl.pallas_call(
        body,
        grid=(t // tm,),
        in_specs=[pl.BlockSpec((tm, a.shape[1]), row), pl.BlockSpec((tm, c.shape[1]), row),
                  pl.BlockSpec((tm, D_MODEL), row)] + [pl.BlockSpec(w.shape, full) for w in weights],
        out_specs=pl.BlockSpec((tm, D_MODEL), row),
        out_shape=jax.ShapeDtypeStruct((t, D_MODEL), F32),
        compiler_params=_params("parallel"),
        name="out_proj",
    )(a, c, res, *weights)


_O_Z = 0
_O_XBC = _O_Z + D_INNER
_O_U = _O_XBC + D_XBC
_O_V = _O_U + D_GMLP
_O_DTE = _O_V + D_GMLP
_O_DT = _O_DTE + D_INNER
_O_END = _O_DT + LANES


def _odd_in_body(x_ref, g_ref, w_ref, z_ref, xbc_ref, u_ref, v_ref, dte_ref, dt_ref):
    h = _rms(x_ref[...], g_ref[...], 1e-6).astype(BF16)
    z_ref[...] = _dot(h, w_ref[:, _O_Z:_O_XBC])
    xbc_ref[...] = _dot(h, w_ref[:, _O_XBC:_O_U])
    u_ref[...] = _dot(h, w_ref[:, _O_U:_O_V])
    v_ref[...] = _dot(h, w_ref[:, _O_V:_O_DTE])
    dte_ref[...] = _dot(h, w_ref[:, _O_DTE:_O_DT])
    dt_ref[...] = _dot(h, w_ref[:, _O_DT:_O_END])


def _odd_in(x, g, w):
    t = x.shape[0]
    tm = _row_tile(t, 512)
    row = lambda i: (i, 0)
    full = lambda i: (0, 0)
    widths = (D_INNER, D_XBC, D_GMLP, D_GMLP, D_INNER, LANES)
    return pl.pallas_call(
        _odd_in_body,
        grid=(t // tm,),
        in_specs=[pl.BlockSpec((tm, D_MODEL), row), pl.BlockSpec((1, D_MODEL), full),
                  pl.BlockSpec(w.shape, full)],
        out_specs=[pl.BlockSpec((tm, n), row) for n in widths],
        out_shape=[jax.ShapeDtypeStruct((t, n), F32) for n in widths],
        compiler_params=_params("parallel"),
        name="odd_in",
    )(x, g, w)


def _split3(x):
    hi = x.astype(BF16)
    r1 = x - hi.astype(F32)
    mid = r1.astype(BF16)
    lo = (r1 - mid.astype(F32)).astype(BF16)
    return hi, mid, lo


def _cumsum_rows(tril_b, x):
    hi, mid, lo = _split3(x)
    return _dot(tril_b, hi) + _dot(tril_b, mid) + _dot(tril_b, lo)


def _ssd_body(z_ref, xbc_ref, dte_ref, dt_ref, prev_ref, h0_ref, cw_ref, cb_ref, dtbe_ref, aloge_ref,
              dtb_ref, alog_ref, de_ref, ng_ref, y_ref, hl_ref, xp_scr, ht_scr, *, rows_in):
    c = pl.program_id(1)
    pairs = D_INNER // LANES

    @pl.when(c == 0)
    def _():
        xp_scr[0:SSM_HALO, :] = prev_ref[...]
        for k in range(pairs):
            ht_scr[:, k * LANES:(k + 1) * LANES] = h0_ref[k * LANES:(k + 1) * LANES, :].T

    def padded(ref):
        v = ref[...]
        if rows_in == CHUNK:
            return v
        return jnp.concatenate([v, jnp.zeros((CHUNK - rows_in, v.shape[1]), F32)], axis=0)

    xp_scr[SSM_HALO:SSM_HALO + CHUNK, :] = padded(xbc_ref)
    first = SSM_HALO - (SSM_CONV_W - 1)
    conv = jnp.broadcast_to(cb_ref[...], (CHUNK, D_XBC))
    for j in range(SSM_CONV_W):
        conv = conv + cw_ref[j:j + 1, :] * xp_scr[first + j:first + j + CHUNK, :]
    xc = _silu(conv)
    xs = xc[:, :D_INNER]
    dt_e = _softplus(padded(dte_ref) + dtbe_ref[...])
    dt_h = _softplus(padded(dt_ref) + dtb_ref[...])
    if rows_in != CHUNK:
        live = lax.broadcasted_iota(jnp.int32, (CHUNK, 1), 0) < rows_in
        xs = jnp.where(live, xs, 0.0)
        dt_e = jnp.where(live, dt_e, 0.0)
        dt_h = jnp.where(live, dt_h, 0.0)

    r_i = lax.broadcasted_iota(jnp.int32, (CHUNK, CHUNK), 0)
    c_i = lax.broadcasted_iota(jnp.int32, (CHUNK, CHUNK), 1)
    causal = r_i >= c_i
    tril_b = jnp.where(causal, 1.0, 0.0).astype(BF16)
    cs_e = _cumsum_rows(tril_b, dt_e * (-jnp.exp(aloge_ref[...])))
    cs_h = _cumsum_rows(tril_b, dt_h * (-jnp.exp(alog_ref[...])))
    cs_t = cs_h.T

    xdt = xs * dt_e
    decay_in = jnp.exp(cs_e)
    xdt_b = xdt.astype(BF16)
    xde_b = (xdt * jnp.exp(cs_e[CHUNK - 1:CHUNK, :] - cs_e)).astype(BF16)
    ht = ht_scr[...]
    ht_b = ht.astype(BF16)
    lane_lo = lax.broadcasted_iota(jnp.int32, (CHUNK, LANES), 1) < P_C
    ys = []
    for g in range(G_C):
        bm = xc[:, D_INNER + g * N_C:D_INNER + (g + 1) * N_C]
        cm = xc[:, D_INNER + (G_C + g) * N_C:D_INNER + (G_C + g + 1) * N_C]
        bm_b = bm.astype(BF16)
        cm_b = cm.astype(BF16)
        cb = _dot_nt(cm_b, bm_b)
        bm_t = bm.T.astype(BF16)
        for k in range(pairs // G_C):
            kk = g * (pairs // G_C) + k
            cols = slice(kk * LANES, (kk + 1) * LANES)
            mats = []
            for hh in (2 * kk, 2 * kk + 1):
                seg = cs_h[:, hh:hh + 1] - cs_t[hh:hh + 1, :]
                mats.append((cb * jnp.exp(jnp.where(causal, seg, NEG_INF))).astype(BF16))
            xp2 = xdt_b[:, cols]
            y_diag = jnp.where(lane_lo, _dot(mats[0], xp2), _dot(mats[1], xp2))
            y_off = _dot(cm_b, ht_b[:, cols]) * decay_in[:, cols]
            ht_scr[:, cols] = ht[:, cols] * decay_in[CHUNK - 1:CHUNK, cols] + _dot(bm_t, xde_b[:, cols])
            ys.append(y_diag + y_off + de_ref[:, cols] * xs[:, cols])
    z = padded(z_ref)
    half = D_INNER // G_C
    outs = []
    for g in range(G_C):
        yg = jnp.concatenate(ys[g * (pairs // G_C):(g + 1) * (pairs // G_C)], axis=1)
        yg = yg * _silu(z[:, g * half:(g + 1) * half])
        outs.append(_rms(yg, ng_ref[:, g * half:(g + 1) * half], 1e-5))
    y_ref[...] = jnp.concatenate(outs, axis=1)[:rows_in]
    xp_scr[0:SSM_HALO, :] = xp_scr[CHUNK:CHUNK + SSM_HALO, :]

    @pl.when(c == pl.num_programs(1) - 1)
    def _():
        for k in range(pairs):
            hl_ref[k * LANES:(k + 1) * LANES, :] = ht_scr[:, k * LANES:(k + 1) * LANES].T


def _ssd(z, xbc, dte, dt, prev, h0, cw, cb, dtbe, aloge, dtb, alog, de, ng, nb, rows_in):
    t = z.shape[0]
    nc = t // nb // rows_in
    body = functools.partial(_ssd_body, rows_in=rows_in)
    row = lambda b, c: (b * nc + c, 0)
    full = lambda b, c: (0, 0)
    seq = lambda b, c: (b, 0, 0)
    vec = lambda a: pl.BlockSpec(a.shape, full)
    return pl.pallas_call(
        body,
        grid=(nb, nc),
        in_specs=[pl.BlockSpec((rows_in, D_INNER), row), pl.BlockSpec((rows_in, D_XBC), row),
                  pl.BlockSpec((rows_in, D_INNER), row), pl.BlockSpec((rows_in, LANES), row),
                  pl.BlockSpec((None, SSM_HALO, D_XBC), seq), pl.BlockSpec((None, D_INNER, N_C), seq),
                  vec(cw), vec(cb), vec(dtbe), vec(aloge), vec(dtb), vec(alog), vec(de), vec(ng)],
        out_specs=[pl.BlockSpec((rows_in, D_INNER), row), pl.BlockSpec((None, D_INNER, N_C), seq)],
        out_shape=[jax.ShapeDtypeStruct((t, D_INNER), F32), jax.ShapeDtypeStruct((nb, D_INNER, N_C), F32)],
        scratch_shapes=[pltpu.VMEM((SSM_HALO + CHUNK, D_XBC), F32), pltpu.VMEM((N_C, D_INNER), F32)],
        compiler_params=_params("parallel", "arbitrary"),
        name="ssd",
    )(z, xbc, dte, dt, prev, h0, cw, cb, dtbe, aloge, dtb, alog, de, ng)


def _gmlp_body(u_ref, v_ref, g_ref, beta_ref, ws_ref, bs_ref, d_ref, vn_ref):
    gu = _gelu_tanh(u_ref[...])
    gv = _gelu_tanh(v_ref[...])
    mu = jnp.mean(gv, axis=-1, keepdims=True)
    cen = gv - mu
    var = jnp.mean(cen * cen, axis=-1, keepdims=True)
    vn = cen * lax.rsqrt(var + 1e-5) * g_ref[...] + beta_ref[...]
    vn_ref[...] = vn
    r_i = lax.broadcasted_iota(jnp.int32, (CHUNK, CHUNK), 0)
    c_i = lax.broadcasted_iota(jnp.int32, (CHUNK, CHUNK), 1)
    dg = D_GMLP // G_D
    for g in range(G_D):
        wm = jnp.where(r_i >= c_i, ws_ref[g], 0.0).astype(BF16)
        s = _dot(wm, vn[:, g * dg:(g + 1) * dg].astype(BF16)) + bs_ref[:, g:g + 1]
        d_ref[:, g * dg:(g + 1) * dg] = gu[:, g * dg:(g + 1) * dg] * s


def _gmlp(u, v, g, beta, ws, bs_t, keep_all_vn):
    t = u.shape[0]
    row = lambda i: (i, 0)
    full = lambda i: (0, 0)
    if keep_all_vn:
        vn_spec, vn_rows = pl.BlockSpec((CHUNK, D_GMLP), row), t
    else:
        vn_spec, vn_rows = pl.BlockSpec((CHUNK, D_GMLP), full), CHUNK
    return pl.pallas_call(
        _gmlp_body,
        grid=(t // CHUNK,),
        in_specs=[pl.BlockSpec((CHUNK, D_GMLP), row), pl.BlockSpec((CHUNK, D_GMLP), row),
                  pl.BlockSpec((1, D_GMLP), full), pl.BlockSpec((1, D_GMLP), full),
                  pl.BlockSpec(ws.shape, lambda i: (0, 0, 0)), pl.BlockSpec(bs_t.shape, full)],
        out_specs=[pl.BlockSpec((CHUNK, D_GMLP), row), vn_spec],
        out_shape=[jax.ShapeDtypeStruct((t, D_GMLP), F32), jax.ShapeDtypeStruct((vn_rows, D_GMLP), F32)],
        compiler_params=_params("arbitrary"),
        name="gmlp",
    )(u, v, g, beta, ws, bs_t)


def _router_body(y_ref, g_ref, wr_ref, br_ref, xn_ref, comb_ref):
    hn = _rms(y_ref[...], g_ref[...], 1e-6)
    xn_ref[...] = hn.astype(BF16)
    logits = jnp.dot(hn, wr_ref[...], preferred_element_type=F32, precision=lax.Precision.HIGHEST) + br_ref[...]
    lane = lax.broadcasted_iota(jnp.int32, logits.shape, 1)
    lane_f = lane.astype(F32)
    far = float(LANES)
    is_g = (lane >= N_EXPERTS) & (lane < N_EXPERTS + E_GROUPS)
    gl = jnp.where(is_g, logits, -jnp.inf)
    gmax = jnp.max(gl, axis=1, keepdims=True)
    g_lane = jnp.min(jnp.where(gl == gmax, lane_f, far), axis=1, keepdims=True)
    g_w = 1.0 / jnp.sum(jnp.where(is_g, jnp.exp(gl - gmax), 0.0), axis=1, keepdims=True)
    g_sel = g_lane.astype(jnp.int32) - N_EXPERTS
    in_grp = (lane < N_EXPERTS) & ((lane // E_PER_GROUP) == g_sel)
    el = jnp.where(in_grp, logits, -jnp.inf)
    v1 = jnp.max(el, axis=1, keepdims=True)
    i1 = jnp.min(jnp.where(el == v1, lane_f, far), axis=1, keepdims=True)
    el2 = jnp.where(lane_f == i1, -jnp.inf, el)
    v2 = jnp.max(el2, axis=1, keepdims=True)
    i2 = jnp.min(jnp.where(el2 == v2, lane_f, far), axis=1, keepdims=True)
    r = jnp.exp(v2 - v1)
    w1 = 1.0 / (1.0 + r)
    w2 = r * w1
    comb_ref[...] = g_w * (jnp.where(lane_f == i1, w1, 0.0) + jnp.where(lane_f == i2, w2, 0.0))


def _router(y, g, wr, br):
    t = y.shape[0]
    tm = _row_tile(t, 512)
    row = lambda i: (i, 0)
    full = lambda i: (0, 0)
    return pl.pallas_call(
        _router_body,
        grid=(t // tm,),
        in_specs=[pl.BlockSpec((tm, D_MODEL), row), pl.BlockSpec((1, D_MODEL), full),
                  pl.BlockSpec(wr.shape, full), pl.BlockSpec((1, LANES), full)],
        out_specs=[pl.BlockSpec((tm, D_MODEL), row), pl.BlockSpec((tm, LANES), row)],
        out_shape=[jax.ShapeDtypeStruct((t, D_MODEL), BF16), jax.ShapeDtypeStruct((t, LANES), F32)],
        compiler_params=_params("parallel"),
        name="router",
    )(y, g, wr, br)


def _moe_dense_body(x_ref, c_ref, res_ref, wg_ref, wu_ref, wd_ref, o_ref):
    e = pl.program_id(1)

    @pl.when(e == 0)
    def _():
        o_ref[...] = res_ref[...]

    x = x_ref[...]
    lane = lax.broadcasted_iota(jnp.int32, c_ref.shape, 1)
    acc = None
    for k in range(MOE_EXPERTS_PER_STEP):
        hg = _dot(x, wg_ref[k])
        hid = (_silu(hg) * _dot(x, wu_ref[k])).astype(BF16)
        ce = jnp.sum(jnp.where(lane == e * MOE_EXPERTS_PER_STEP + k, c_ref[...], 0.0), axis=1, keepdims=True)
        y = ce * _dot(hid, wd_ref[k])
        acc = y if acc is None else acc + y
    o_ref[...] += acc


def _moe_dense(xn, comb, res, wg, wu, wd):
    t = xn.shape[0]
    tm = _row_tile(t, 1024)
    row = lambda i, e: (i, 0)
    eg = MOE_EXPERTS_PER_STEP
    return pl.pallas_call(
        _moe_dense_body,
        grid=(t // tm, N_EXPERTS // eg),
        in_specs=[pl.BlockSpec((tm, D_MODEL), row), pl.BlockSpec((tm, LANES), row),
                  pl.BlockSpec((tm, D_MODEL), row),
                  pl.BlockSpec((eg, D_MODEL, D_EXPERT), lambda i, e: (e, 0, 0)),
                  pl.BlockSpec((eg, D_MODEL, D_EXPERT), lambda i, e: (e, 0, 0)),
                  pl.BlockSpec((eg, D_EXPERT, D_MODEL), lambda i, e: (e, 0, 0))],
        out_specs=pl.BlockSpec((tm, D_MODEL), row),
        out_shape=jax.ShapeDtypeStruct((t, D_MODEL), F32),
        compiler_params=_params("parallel", "arbitrary"),
        name="moe_dense",
    )(xn, comb, res, wg, wu, wd)


def _final_norm_body(x_ref, g_ref, o_ref):
    o_ref[...] = _rms(x_ref[...], g_ref[...], 1e-6)


def _final_norm(x, g):
    t = x.shape[0]
    tm = _row_tile(t, 1024)
    return pl.pallas_call(
        _final_norm_body,
        grid=(t // tm,),
        in_specs=[pl.BlockSpec((tm, D_MODEL), lambda i: (i, 0)), pl.BlockSpec((1, D_MODEL), lambda i: (0, 0))],
        out_specs=pl.BlockSpec((tm, D_MODEL), lambda i: (i, 0)),
        out_shape=jax.ShapeDtypeStruct((t, D_MODEL), F32),
        compiler_params=_params("parallel"),
        name="final_norm",
    )(x, g)


def _row(v):
    return v.reshape(1, -1).astype(F32)


def _pad_lanes(v, n=LANES):
    return jnp.pad(v, [(0, 0)] * (v.ndim - 1) + [(0, n - v.shape[-1])])


def kernel(x_prompt, x_sample, cache_attn_k, cache_attn_v, state_conf_conv, state_ssm_conv, state_ssm, page_table,
           norm_mix_g, norm_moe_g, norm_final_g,
           even_w_in, attn_lam_q1, attn_lam_k1, attn_lam_q2, attn_lam_k2, attn_subln_g,
           conf_conv_w, conf_conv_b, conf_ln_g, conf_ln_b, even_w_out,
           odd_w_in, ssm_conv_w, ssm_conv_b, ssm_dt_bias, ssm_a_log, ssm_d, ssm_norm_g,
           gmlp_ln_g, gmlp_ln_b, gmlp_ws, gmlp_bs, odd_w_out,
           moe_rg_w, moe_rg_b, moe_re_w, moe_re_b, moe_w_gate, moe_w_up, moe_w_down):
    bp, lp, _ = x_prompt.shape
    db, ls, _ = x_sample.shape
    assert bp == 1 and lp % CHUNK == 0 and (db * ls) % CHUNK == 0 and CHUNK % ls == 0
    depth = norm_mix_g.shape[0]
    page = cache_attn_k.shape[2]
    yp = x_prompt.reshape(bp * lp, D_MODEL)
    ys = x_sample.reshape(db * ls, D_MODEL)
    outs_p = {k: [] for k in ("k", "v", "conf", "sconv", "ssm", "gmlp")}
    outs_s = {k: [] for k in ("k", "v", "conf", "sconv", "ssm", "gmlp")}

    for layer in range(depth):
        g_mix = _row(norm_mix_g[layer])
        if layer % 2 == 0:
            e = layer // 2
            lambda_init = 0.8 - 0.6 * math.exp(-0.3 * layer)
            w_in = _split_weight(even_w_in[e].astype(F32))
            w_out = (_split_weight(even_w_out[e, :D_ATT].astype(F32))
                     + _split_weight(even_w_out[e, D_ATT:].astype(F32)))
            lamv = _pad_lanes(jnp.stack([attn_lam_q1[e], attn_lam_k1[e], attn_lam_q2[e], attn_lam_k2[e]]).astype(F32))
            lamv = jnp.pad(lamv, [(0, 4), (0, 0)])
            subln = _row(attn_subln_g[e])
            cw = jnp.pad(conf_conv_w[e].astype(F32), [(0, CONF_HALO - CONF_W), (0, 0)])
            cb, lg, lb = _row(conf_conv_b[e]), _row(conf_ln_g[e]), _row(conf_ln_b[e])

            qx, kx, k, v, vx, glu = _even_in(yp, g_mix, *w_in)
            attn = _attn_prompt(lamv, qx, kx, vx, subln, lambda_init)
            conv = _conf_conv(glu, glu, cw, cb, lg, lb, _row_tile(lp, 256), True)
            yp = _out_proj(attn, conv, yp, *w_out)
            outs_p["k"].append(k.reshape(bp, lp, H_A, HEAD_W))
            outs_p["v"].append(v.reshape(bp, lp, H_A, HEAD_W))
            outs_p["conf"].append(glu[lp - (CONF_W - 1):].reshape(bp, CONF_W - 1, D_CONF))

            qx, kx, k, v, _, glu = _even_in(ys, g_mix, *w_in)
            qparts = qx.astype(F32).reshape(2, db, ls, H_A, 4, DK_A)
            zero = jnp.zeros((db, ls, H_A, DK_A), F32)

            def q_rows(part):
                both = jnp.stack([jnp.concatenate([qparts[0, ..., part, :], zero], axis=-1),
                                  jnp.concatenate([zero, qparts[1, ..., part, :]], axis=-1)], axis=1)
                return both.transpose(0, 1, 3, 2, 4).reshape(db * 2 * H_A * ls, HEAD_W)

            n_pool = cache_attn_k.shape[1]
            attn = _attn_sample(page_table + e * n_pool, lamv, q_rows(0), q_rows(2),
                                k.reshape(db * ls * H_A, HEAD_W), v.reshape(db * ls * H_A, HEAD_W), subln,
                                cache_attn_k.reshape(-1, page * H_A, HEAD_W),
                                cache_attn_v.reshape(-1, page * H_A, HEAD_W), ls, lambda_init)
            hist = state_conf_conv[e].astype(F32)
            halo = jnp.pad(hist, [(0, 0), (CONF_HALO - (CONF_W - 1), 0), (0, 0)])
            conv = _conf_conv(glu, halo, cw, cb, lg, lb, ls, False)
            ys = _out_proj(attn, conv, ys, *w_out)
            outs_s["k"].append(k.reshape(db, ls, H_A, HEAD_W))
            outs_s["v"].append(v.reshape(db, ls, H_A, HEAD_W))
            outs_s["conf"].append(jnp.concatenate([hist, glu.reshape(db, ls, D_CONF)], axis=1)[:, ls:])
        else:
            o = layer // 2
            w = odd_w_in[o]
            i1, i2, i3, i4 = D_INNER, D_INNER + D_XBC, D_INNER + D_XBC + H_C, D_INNER + D_XBC + H_C + D_GMLP
            w_dt = w[:, i2:i3]
            w_in = jnp.concatenate([w[:, :i1], w[:, i1:i2], w[:, i3:i4], w[:, i4:],
                                    jnp.repeat(w_dt, P_C, axis=1), _pad_lanes(w_dt)], axis=1).astype(BF16)
            wy = odd_w_out[o, :D_INNER].astype(BF16)
            wd_ = odd_w_out[o, D_INNER:].astype(BF16)
            cw = jnp.pad(ssm_conv_w[o].astype(F32), [(0, SSM_HALO - SSM_CONV_W), (0, 0)])
            cb = _row(ssm_conv_b[o])
            dtb, alog = _pad_lanes(_row(ssm_dt_bias[o])), _pad_lanes(_row(ssm_a_log[o]))
            dtbe, aloge = _row(jnp.repeat(ssm_dt_bias[o], P_C)), _row(jnp.repeat(ssm_a_log[o], P_C))
            de = _row(jnp.repeat(ssm_d[o], P_C))
            ng = _row(ssm_norm_g[o])
            lg, lb = _row(gmlp_ln_g[o]), _row(gmlp_ln_b[o])
            ws = gmlp_ws[o].astype(F32)
            bs_t = _pad_lanes(gmlp_bs[o].astype(F32).T)

            z, xbc, u, v, dte, dt = _odd_in(yp, g_mix, w_in)
            prev0 = jnp.zeros((bp, SSM_HALO, D_XBC), F32)
            h00 = jnp.zeros((bp, D_INNER, N_C), F32)
            y, h_new = _ssd(z, xbc, dte, dt, prev0, h00, cw, cb, dtbe, aloge, dtb, alog, de, ng, bp, CHUNK)
            d_out, vn = _gmlp(u, v, lg, lb, ws, bs_t, False)
            yp = _out_proj(y, d_out, yp, wy, wd_)
            outs_p["sconv"].append(xbc[lp - (SSM_CONV_W - 1):].reshape(bp, SSM_CONV_W - 1, D_XBC))
            outs_p["ssm"].append(h_new.reshape(bp, H_C, P_C, N_C))
            outs_p["gmlp"].append(vn.reshape(bp, CHUNK, D_GMLP))

            z, xbc, u, v, dte, dt = _odd_in(ys, g_mix, w_in)
            hist = state_ssm_conv[o].astype(F32)
            prev = jnp.pad(hist, [(0, 0), (SSM_HALO - (SSM_CONV_W - 1), 0), (0, 0)])
            y, h_new = _ssd(z, xbc, dte, dt, prev, state_ssm[o].astype(F32).reshape(db, D_INNER, N_C),
                            cw, cb, dtbe, aloge, dtb, alog, de, ng, db, ls)
            eye = jnp.eye(CHUNK // ls, dtype=F32)
            ws_s = jnp.stack([jnp.kron(eye, ws[g, :ls, :ls]) for g in range(G_D)])
            bs_s = jnp.tile(bs_t[:ls], (CHUNK // ls, 1))
            d_out, vn = _gmlp(u, v, lg, lb, ws_s, bs_s, True)
            ys = _out_proj(y, d_out, ys, wy, wd_)
            outs_s["sconv"].append(jnp.concatenate([hist, xbc.reshape(db, ls, D_XBC)], axis=1)[:, ls:])
            outs_s["ssm"].append(h_new.reshape(db, H_C, P_C, N_C))
            outs_s["gmlp"].append(vn.reshape(db, ls, D_GMLP))

        g_moe = _row(norm_moe_g[layer])
        wr = _pad_lanes(jnp.concatenate([moe_re_w[layer], moe_rg_w[layer]], axis=1).astype(F32))
        br = _pad_lanes(_row(jnp.concatenate([moe_re_b[layer], moe_rg_b[layer]])))
        wg = moe_w_gate[layer].astype(BF16)
        wu = moe_w_up[layer].astype(BF16)
        wd = moe_w_down[layer].astype(BF16)
        xn, comb = _router(yp, g_moe, wr, br)
        yp = _moe_dense(xn, comb, yp, wg, wu, wd)
        xn, comb = _router(ys, g_moe, wr, br)
        ys = _moe_dense(xn, comb, ys, wg, wu, wd)

    g_fin = _row(norm_final_g)
    y_prompt = _final_norm(yp, g_fin).reshape(bp, lp, D_MODEL)
    y_sample = _final_norm(ys, g_fin).reshape(db, ls, D_MODEL)
    st = lambda xs: jnp.stack(xs)
    return (y_prompt, y_sample,
            st(outs_p["k"]), st(outs_p["v"]), st(outs_p["conf"]), st(outs_p["sconv"]), st(outs_p["ssm"]),
            st(outs_p["gmlp"]),
            st(outs_s["k"]), st(outs_s["v"]), st(outs_s["conf"]), st(outs_s["sconv"]), st(outs_s["ssm"]),
            st(outs_s["gmlp"]))
```
